```python
import math, functools
import jax, jax.numpy as jnp
from jax import lax
import numpy as np

D_MODEL = 1024
BATCH = 16
SEQ = 2048
DEPTH = 1
DEC_BATCH = 32
DEC_SEQ = 8
PAST_LEN = 16384
PAGE_SIZE = 128

P_DIM = 256
H_A = 4
DH_A = 64
E_A = 2 * DH_A
H_R = 4
K_R = 128
V_R = 128
HGRN_CHUNK = 32
Q_BLOCK = 128
N_GROUPS = 4
EXPERTS_PER_GROUP = 4
N_EXPERTS = N_GROUPS * EXPERTS_PER_GROUP
TOP_K = 2
D_EXPERT = 256
ROPE_THETA = 10000.0
EPS = 1e-6
NEG_INF = -1e30
ATT_WIDTH = H_A * E_A
REC_WIDTH = H_R * V_R
IN_WIDTHS = (2 * H_A * DH_A, 2 * H_A * DH_A, H_A * E_A, H_R * K_R, H_R * K_R, H_R * V_R, H_R * V_R, D_MODEL, D_MODEL)
D_IN = sum(IN_WIDTHS)

kernel_name = 'diffattn_hgrn2_hmoe_hybrid_step'


def _rmsnorm(x, g):
    xf = x.astype(jnp.float32)
    y = xf * lax.rsqrt(jnp.mean(xf * xf, axis=-1, keepdims=True) + EPS) * g.astype(jnp.float32)
    return y.astype(x.dtype)


def _rope(x, pos):
    half = DH_A // 2
    inv_freq = ROPE_THETA ** (-jnp.arange(half, dtype=jnp.float32) / half)
    ang = pos.astype(jnp.float32)[:, None] * inv_freq[None, :]
    cos = jnp.cos(ang)[:, None, None, :]
    sin = jnp.sin(ang)[:, None, None, :]
    xf = x.astype(jnp.float32)
    x1, x2 = xf[..., :half], xf[..., half:]
    return jnp.concatenate([x1 * cos - x2 * sin, x2 * cos + x1 * sin], axis=-1).astype(x.dtype)


def _online_softmax_update(carry, s, v):
    m, l, acc = carry
    m_new = jnp.maximum(m, s.max(-1))
    alpha = jnp.exp(m - m_new)
    p = jnp.exp(s - m_new[..., None])
    l = l * alpha + p.sum(-1)
    acc = acc * alpha[..., None] + jnp.einsum('bhcts,bshe->bhcte', p, v.astype(jnp.float32))
    return m_new, l, acc


def _attend_prompt(q, k, v):
    B, S = q.shape[:2]
    nb = S // Q_BLOCK
    scale = DH_A ** -0.5
    qb = q.reshape(B, nb, Q_BLOCK, H_A, 2, DH_A).transpose(1, 0, 2, 3, 4, 5)
    key_pos = jnp.arange(S)
    vf = v.astype(jnp.float32)

    def block(args):
        qi, start = args
        s = jnp.einsum('bthcd,bshcd->bhcts', qi, k).astype(jnp.float32) * scale
        q_pos = start + jnp.arange(Q_BLOCK)
        s = jnp.where(key_pos[None, :] <= q_pos[:, None], s, NEG_INF)
        p = jax.nn.softmax(s, axis=-1)
        return jnp.einsum('bhcts,bshe->bhcte', p, vf)

    o = lax.map(block, (qb, jnp.arange(nb) * Q_BLOCK))
    return o.transpose(1, 2, 3, 0, 4, 5).reshape(B, H_A, 2, S, E_A)


def _attend_sample(q, k_new, v_new, cache_k, cache_v, page_table, layer):
    B, T = q.shape[:2]
    scale = DH_A ** -0.5
    m0 = jnp.full((B, H_A, 2, T), NEG_INF, jnp.float32)
    l0 = jnp.zeros((B, H_A, 2, T), jnp.float32)
    a0 = jnp.zeros((B, H_A, 2, T, E_A), jnp.float32)

    def page_step(carry, pages):
        kp = cache_k[layer, pages]
        vp = cache_v[layer, pages]
        s = jnp.einsum('bthcd,bshcd->bhcts', q, kp).astype(jnp.float32) * scale
        return _online_softmax_update(carry, s, vp), None

    carry, _ = lax.scan(page_step, (m0, l0, a0), page_table.T)
    s = jnp.einsum('bthcd,bshcd->bhcts', q, k_new).astype(jnp.float32) * scale
    s = jnp.where(jnp.tril(jnp.ones((T, T), bool)), s, NEG_INF)
    m, l, acc = _online_softmax_update(carry, s, v_new)
    return acc / l[..., None]


def _hgrn2(q, f_logit, v, lb, s0):
    B, T = q.shape[:2]
    C = math.gcd(T, HGRN_CHUNK)
    n = T // C
    f = lb + (1.0 - lb) * jax.nn.sigmoid(f_logit.astype(jnp.float32))
    g = jnp.log(f)
    k = 1.0 - f

    def chunks(a):
        return a.astype(jnp.float32).reshape(B, n, C, H_R, a.shape[-1]).transpose(1, 0, 3, 2, 4)

    tril = jnp.tril(jnp.ones((C, C), bool))

    def step(S, xs):
        qc, kc, vc, gc = xs
        b = jnp.cumsum(gc, axis=2)
        qe = qc * jnp.exp(b)
        A = jnp.where(tril, jnp.einsum('bhtk,bhsk->bhts', qe, kc * jnp.exp(-b)), 0.0)
        o = jnp.einsum('bhts,bhsv->bhtv', A, vc) + jnp.einsum('bhtk,bhkv->bhtv', qe, S)
        b_last = b[:, :, -1:, :]
        S = jnp.exp(b_last[:, :, 0, :])[..., None] * S + jnp.einsum('bhsk,bhsv->bhkv', kc * jnp.exp(b_last - b), vc)
        return S, o

    S, o = lax.scan(step, s0.astype(jnp.float32), (chunks(q), chunks(k), chunks(v), chunks(g)))
    return o.transpose(1, 0, 3, 2, 4).reshape(B, T, H_R, V_R), S


def _hier_moe(h, w_rg, b_rg, w_re, b_re, w_gate, w_up, w_down):
    pg = jax.nn.softmax((h @ w_rg + b_rg).astype(jnp.float32), axis=-1)
    p_top, g_idx = lax.top_k(pg, 1)
    le = (h @ w_re + b_re).astype(jnp.float32).reshape(-1, N_GROUPS, EXPERTS_PER_GROUP)
    le = jnp.einsum('ng,nge->ne', jax.nn.one_hot(g_idx[:, 0], N_GROUPS, dtype=jnp.float32), le)
    v2, e_idx = lax.top_k(le, TOP_K)
    w2 = jax.nn.softmax(v2, axis=-1) * p_top
    expert_id = g_idx * EXPERTS_PER_GROUP + e_idx
    combine = jnp.einsum('nk,nke->ne', w2, jax.nn.one_hot(expert_id, N_EXPERTS, dtype=jnp.float32)).astype(h.dtype)
    y = jnp.zeros_like(h)
    for grp in range(N_GROUPS):
        sl = slice(grp * EXPERTS_PER_GROUP, (grp + 1) * EXPERTS_PER_GROUP)
        a = jnp.einsum('nd,edf->nef', h, w_gate[sl])
        u = jnp.einsum('nd,edf->nef', h, w_up[sl])
        hid = jax.nn.silu(a) * u * combine[:, sl, None]
        y = y + jnp.einsum('nef,efd->nd', hid, w_down[sl])
    return y


def _decoder_layer(i, x, p, pos, attend, s0, g_mix, w_in, lam, g_subln, lb_param, g_rec,
                   w_branch_a, w_branch_r, w_out, g_ffn, w_route_group, b_route_group,
                   w_route_expert, b_route_expert, w_exp_gate, w_exp_up, w_exp_down,
                   g_ple, w_ple_gate, w_ple):
    B, T, _ = x.shape
    h = _rmsnorm(x, g_mix[i])
    proj = h @ w_in[i]
    q_a, k_a, v_a, q_r, f_r, i_r, g_r, gate_a, gate_r = jnp.split(proj, list(np.cumsum(IN_WIDTHS)[:-1]), axis=-1)

    q_a = _rope(q_a.reshape(B, T, H_A, 2, DH_A), pos)
    k_a = _rope(k_a.reshape(B, T, H_A, 2, DH_A), pos)
    v_a = v_a.reshape(B, T, H_A, E_A)
    o2 = attend(q_a, k_a, v_a)
    lam_init = 0.8 - 0.6 * math.exp(-0.3 * i)
    lamf = lam[i].astype(jnp.float32)
    lam_val = jnp.exp(jnp.sum(lamf[0] * lamf[1])) - jnp.exp(jnp.sum(lamf[2] * lamf[3])) + lam_init
    od = (o2[:, :, 0] - lam_val * o2[:, :, 1]).transpose(0, 2, 1, 3)
    y_a = (_rmsnorm(od, g_subln[i]) * (1.0 - lam_init)).reshape(B, T, ATT_WIDTH).astype(x.dtype)

    lb = jnp.cumsum(jax.nn.softmax(lb_param.astype(jnp.float32), axis=0), axis=0)[i].reshape(H_R, K_R)
    o_r, s_new = _hgrn2(jax.nn.silu(q_r).reshape(B, T, H_R, K_R), f_r.reshape(B, T, H_R, K_R),
                        i_r.reshape(B, T, H_R, V_R), lb, s0)
    y_r = _rmsnorm(o_r, g_rec[i]) * jax.nn.silu(g_r.reshape(B, T, H_R, V_R).astype(jnp.float32))
    y_r = y_r.reshape(B, T, REC_WIDTH).astype(x.dtype)

    merged = jax.nn.sigmoid(gate_a) * (y_a @ w_branch_a[i]) + jax.nn.sigmoid(gate_r) * (y_r @ w_branch_r[i])
    x = x + merged @ w_out[i]

    hf = _rmsnorm(x, g_ffn[i]).reshape(B * T, D_MODEL)
    x = x + _hier_moe(hf, w_route_group[i], b_route_group[i], w_route_expert[i], b_route_expert[i],
                      w_exp_gate[i], w_exp_up[i], w_exp_down[i]).reshape(B, T, D_MODEL)

    hp = _rmsnorm(x, g_ple[i])
    x = x + jax.nn.sigmoid(hp @ w_ple_gate[i]) * (p[i] @ w_ple[i])
    return x, k_a, v_a, s_new


def setup_inputs(seed: int = 0) -> dict:
    key = jax.random.key(seed)
    ks = jax.random.split(key, 32)
    n_pages = PAST_LEN // PAGE_SIZE
    n_used = DEC_BATCH * n_pages
    n_phys = n_used + (n_used + 3) // 4
    f32 = jnp.float32

    def nrm(k, shape, scale):
        return jax.random.normal(k, shape, f32) * scale

    def gain(k, shape):
        return 1.0 + 0.02 * jax.random.normal(k, shape, f32)

    perm = jax.random.permutation(ks[7], n_phys)[:n_used]
    return {
        'x_prompt': nrm(ks[0], (BATCH, SEQ, D_MODEL), 1.0),
        'x_sample': nrm(ks[1], (DEC_BATCH, DEC_SEQ, D_MODEL), 1.0),
        'p_prompt': nrm(ks[2], (DEPTH, BATCH, SEQ, P_DIM), 1.0),
        'p_sample': nrm(ks[3], (DEPTH, DEC_BATCH, DEC_SEQ, P_DIM), 1.0),
        'cache_k': nrm(ks[4], (DEPTH, n_phys, PAGE_SIZE, H_A, 2, DH_A), 1.0),
        'cache_v': nrm(ks[5], (DEPTH, n_phys, PAGE_SIZE, H_A, E_A), 1.0),
        'state_hgrn': nrm(ks[6], (DEPTH, DEC_BATCH, H_R, K_R, V_R), 0.5),
        'page_table': perm.reshape(DEC_BATCH, n_pages).astype(jnp.int32),
        'g_mix': gain(ks[8], (DEPTH, D_MODEL)),
        'w_in': nrm(ks[9], (DEPTH, D_MODEL, D_IN), D_MODEL ** -0.5),
        'lam': nrm(ks[10], (DEPTH, 4, DH_A), 0.1),
        'g_subln': gain(ks[11], (DEPTH, E_A)),
        'lb_param': nrm(ks[12], (DEPTH + 1, H_R * K_R), 0.1),
        'g_rec': gain(ks[13], (DEPTH, V_R)),
        'w_branch_a': nrm(ks[14], (DEPTH, ATT_WIDTH, D_MODEL), ATT_WIDTH ** -0.5),
        'w_branch_r': nrm(ks[15], (DEPTH, REC_WIDTH, D_MODEL), REC_WIDTH ** -0.5),
        'w_out': nrm(ks[16], (DEPTH, D_MODEL, D_MODEL), D_MODEL ** -0.5),
        'g_ffn': gain(ks[17], (DEPTH, D_MODEL)),
        'w_route_group': nrm(ks[18], (DEPTH, D_MODEL, N_GROUPS), D_MODEL ** -0.5),
        'b_route_group': nrm(ks[19], (DEPTH, N_GROUPS), 0.01),
        'w_route_expert': nrm(ks[20], (DEPTH, D_MODEL, N_EXPERTS), D_MODEL ** -0.5),
        'b_route_expert': nrm(ks[21], (DEPTH, N_EXPERTS), 0.01),
        'w_exp_gate': nrm(ks[22], (DEPTH, N_EXPERTS, D_MODEL, D_EXPERT), D_MODEL ** -0.5),
        'w_exp_up': nrm(ks[23], (DEPTH, N_EXPERTS, D_MODEL, D_EXPERT), D_MODEL ** -0.5),
        'w_exp_down': nrm(ks[24], (DEPTH, N_EXPERTS, D_EXPERT, D_MODEL), D_EXPERT ** -0.5),
        'g_ple': gain(ks[25], (DEPTH, D_MODEL)),
        'w_ple_gate': nrm(ks[26], (DEPTH, D_MODEL, D_MODEL), D_MODEL ** -0.5),
        'w_ple': nrm(ks[27], (DEPTH, P_DIM, D_MODEL), P_DIM ** -0.5),
        'g_final': gain(ks[28], (D_MODEL,)),
    }


def reference(x_prompt, x_sample, p_prompt, p_sample, cache_k, cache_v, state_hgrn, page_table,
              g_mix, w_in, lam, g_subln, lb_param, g_rec, w_branch_a, w_branch_r, w_out, g_ffn,
              w_route_group, b_route_group, w_route_expert, b_route_expert, w_exp_gate, w_exp_up,
              w_exp_down, g_ple, w_ple_gate, w_ple, g_final):
    B, S = x_prompt.shape[:2]
    T = x_sample.shape[1]
    pos_p = jnp.arange(S, dtype=jnp.int32)
    pos_s = PAST_LEN + jnp.arange(T, dtype=jnp.int32)
    weights = (g_mix, w_in, lam, g_subln, lb_param, g_rec, w_branch_a, w_branch_r, w_out, g_ffn,
               w_route_group, b_route_group, w_route_expert, b_route_expert, w_exp_gate, w_exp_up,
               w_exp_down, g_ple, w_ple_gate, w_ple)
    s0_prompt = jnp.zeros((B, H_R, K_R, V_R), jnp.float32)
    xp, xs = x_prompt, x_sample
    kp_l, vp_l, sp_l, ks_l, vs_l, ss_l = [], [], [], [], [], []
    for i in range(DEPTH):
        attend_s = functools.partial(_attend_sample, cache_k=cache_k, cache_v=cache_v,
                                     page_table=page_table, layer=i)
        xp, kp, vp, sp = _decoder_layer(i, xp, p_prompt, pos_p, _attend_prompt, s0_prompt, *weights)
        xs, k_s, v_s, s_s = _decoder_layer(i, xs, p_sample, pos_s, attend_s, state_hgrn[i], *weights)
        kp_l.append(kp); vp_l.append(vp); sp_l.append(sp)
        ks_l.append(k_s); vs_l.append(v_s); ss_l.append(s_s)
    y_prompt = _rmsnorm(xp, g_final)
    y_sample = _rmsnorm(xs, g_final)
    return (y_prompt, y_sample, jnp.stack(kp_l), jnp.stack(vp_l), jnp.stack(sp_l),
            jnp.stack(ks_l), jnp.stack(vs_l), jnp.stack(ss_l))
```

```python
import functools
import math

import jax
import jax.numpy as jnp
import numpy as np
from jax import lax
from jax.experimental import pallas as pl
from jax.experimental.pallas import tpu as pltpu

F32 = jnp.float32
BF16 = jnp.bfloat16
I32 = jnp.int32

D_MODEL = 1024
P_DIM = 256
H_A = 4
DH_A = 64
E_A = 2 * DH_A
H_R = 4
K_R = 128
V_R = 128
HGRN_CHUNK = 32
N_GROUPS = 4
EXPERTS_PER_GROUP = 4
N_EXPERTS = N_GROUPS * EXPERTS_PER_GROUP
D_EXPERT = 256
ROPE_THETA = 10000.0
EPS = 1e-6
NEG_INF = -1e30
PAGE_SIZE = 128
ATT_W = H_A * E_A
REC_W = H_R * V_R
N_SEQ_COLS = 7 * 512
LANES = 128
N_PAIRS = 6
N_CLASSES = N_GROUPS * N_PAIRS
PAIR_LO = (0, 0, 0, 1, 1, 2)
PAIR_HI = (1, 2, 3, 2, 3, 3)
ROW_W = D_MODEL + LANES
TOKEN_TILE = 256
ATTN_TILE = 256
PAGES_PER_STEP = 8
VMEM_LIMIT = 56 * 1024 * 1024


def _cparams(n_axes):
    return pltpu.CompilerParams(dimension_semantics=("arbitrary",) * n_axes, vmem_limit_bytes=VMEM_LIMIT)


def _rms(x, g):
    return x * lax.rsqrt(jnp.mean(x * x, axis=-1, keepdims=True) + EPS) * g


def _silu(x):
    return x * jax.nn.sigmoid(x)


def _dot(a, b):
    return jnp.dot(a, b, preferred_element_type=F32)


def _dot_nt(a, b):
    return lax.dot_general(a, b, (((1,), (1,)), ((), ())), preferred_element_type=F32)


def _inproj_kernel(x_ref, g_ref, w_ref, cos_ref, sin_ref,
                   q_ref, kf_ref, kb_ref, vf_ref, vb_ref, qr_ref, fl_ref, ir_ref, gr_ref):
    h = _rms(x_ref[...], g_ref[...]).astype(BF16)
    cos = cos_ref[...]
    sin = sin_ref[...]
    tm = h.shape[0]
    lane = lax.broadcasted_iota(I32, (tm, ATT_W), 1)
    first_half = (lane % DH_A) < (DH_A // 2)

    def proj(c):
        return _dot(h, w_ref[:, c * 512:(c + 1) * 512])

    def rope(t):
        rot = jnp.where(first_half, pltpu.roll(t, ATT_W - DH_A // 2, 1), pltpu.roll(t, DH_A // 2, 1))
        return t * cos + rot * sin

    q = rope(proj(0))
    q_ref[...] = (q * (DH_A ** -0.5)).astype(q_ref.dtype)
    k = rope(proj(1))
    kf_ref[...] = k
    kb_ref[...] = k.astype(kb_ref.dtype)
    v = proj(2)
    vf_ref[...] = v
    vb_ref[...] = v.astype(vb_ref.dtype)
    qr_ref[...] = _silu(proj(3))
    fl_ref[...] = proj(4)
    ir_ref[...] = proj(5)
    gr_ref[...] = proj(6)


def _inproj(x, g_mix, w_seq, cos_t, sin_t, seq, small_dtype):
    n = x.shape[0]
    tm = TOKEN_TILE
    nblk = max(seq // tm, 1)
    row = lambda i: (i, 0)
    tab = lambda i: (i % nblk, 0)
    const = lambda i: (0, 0)
    o512 = pl.BlockSpec((tm, 512), row)
    sds = lambda dt: jax.ShapeDtypeStruct((n, 512), dt)
    return pl.pallas_call(
        _inproj_kernel,
        grid=(n // tm,),
        in_specs=[pl.BlockSpec((tm, D_MODEL), row), pl.BlockSpec((1, D_MODEL), const),
                  pl.BlockSpec((D_MODEL, N_SEQ_COLS), const),
                  pl.BlockSpec((tm, ATT_W), tab), pl.BlockSpec((tm, ATT_W), tab)],
        out_specs=[o512] * 9,
        out_shape=[sds(small_dtype), sds(F32), sds(small_dtype), sds(F32), sds(small_dtype),
                   sds(F32), sds(F32), sds(F32), sds(F32)],
        compiler_params=_cparams(1),
        name="inproj",
    )(x, g_mix, w_seq, cos_t, sin_t)


def _lambda_value(lam_ref):
    lam = lam_ref[...]
    s01 = jnp.sum(lam[0:1, :] * lam[1:2, :], axis=-1, keepdims=True)
    s23 = jnp.sum(lam[2:3, :] * lam[3:4, :], axis=-1, keepdims=True)
    return jnp.exp(s01) - jnp.exp(s23)


def _diff_norm(o0, o1, lam_val, g_sub, lam_init):
    od = o0 - lam_val * o1
    return _rms(od, g_sub) * (1.0 - lam_init)


def _split_maps(qh):
    lane = lax.broadcasted_iota(I32, qh.shape, 1)
    zero = jnp.zeros_like(qh)
    return jnp.concatenate([jnp.where(lane < DH_A, qh, zero), jnp.where(lane >= DH_A, qh, zero)], axis=0)


def _attn_prompt_kernel(q_ref, k_ref, v_ref, lam_ref, g_ref, o_ref, m_scr, l_scr, acc_scr, *, lam_init):
    i = pl.program_id(1)
    tq = q_ref.shape[0]
    lam_val = _lambda_value(lam_ref) + lam_init
    row = lax.broadcasted_iota(I32, (2 * tq, tq), 0) % tq
    col = lax.broadcasted_iota(I32, (2 * tq, tq), 1)
    above_diag = col > row
    for h in range(H_A):
        cs = slice(h * E_A, (h + 1) * E_A)
        q2 = _split_maps(q_ref[:, cs])
        m_scr[...] = jnp.full(m_scr.shape, NEG_INF, F32)
        l_scr[...] = jnp.zeros(l_scr.shape, F32)
        acc_scr[...] = jnp.zeros(acc_scr.shape, F32)

        def body(kb, carry):
            r0 = pl.multiple_of(kb * tq, tq)
            kh = k_ref[pl.ds(r0, tq), cs]
            vh = v_ref[pl.ds(r0, tq), cs]
            s = _dot_nt(q2, kh)
            s = jnp.where(jnp.logical_and(kb == i, above_diag), NEG_INF, s)
            m_old = m_scr[...]
            m_new = jnp.maximum(m_old, jnp.max(s, axis=-1, keepdims=True))
            alpha = jnp.exp(m_old - m_new)
            p = jnp.exp(s - m_new)
            l_scr[...] = l_scr[...] * alpha + jnp.sum(p, axis=-1, keepdims=True)
            acc_scr[...] = acc_scr[...] * alpha + _dot(p.astype(BF16), vh)
            m_scr[...] = m_new
            return carry

        lax.fori_loop(0, i + 1, body, 0)
        o = acc_scr[...] / l_scr[...]
        y = _diff_norm(o[:tq], o[tq:], lam_val, g_ref[...], lam_init)
        o_ref[:, cs] = y.astype(o_ref.dtype)


def _attn_prompt(q, k, v, lam, g_sub, batch, seq, lam_init):
    tq = ATTN_TILE
    nq = seq // tq
    return pl.pallas_call(
        functools.partial(_attn_prompt_kernel, lam_init=lam_init),
        grid=(batch, nq),
        in_specs=[pl.BlockSpec((tq, ATT_W), lambda b, i: (b * nq + i, 0)),
                  pl.BlockSpec((seq, ATT_W), lambda b, i: (b, 0)),
                  pl.BlockSpec((seq, ATT_W), lambda b, i: (b, 0)),
                  pl.BlockSpec((4, DH_A), lambda b, i: (0, 0)),
                  pl.BlockSpec((1, E_A), lambda b, i: (0, 0))],
        out_specs=pl.BlockSpec((tq, ATT_W), lambda b, i: (b * nq + i, 0)),
        out_shape=jax.ShapeDtypeStruct(q.shape, BF16),
        scratch_shapes=[pltpu.VMEM((2 * tq, 1), F32), pltpu.VMEM((2 * tq, 1), F32),
                        pltpu.VMEM((2 * tq, E_A), F32)],
        compiler_params=_cparams(2),
        name="attn_prompt",
    )(q, k, v, lam, g_sub)


def _attn_sample_kernel(pt_ref, q_ref, kn_ref, vn_ref, lam_ref, g_ref, *refs, lam_init, pages):
    k_refs = refs[:pages]
    v_refs = refs[pages:2 * pages]
    o_ref = refs[2 * pages]
    m_scr, l_scr, acc_scr = refs[2 * pages + 1:]
    j = pl.program_id(1)
    t = q_ref.shape[0]

    @pl.when(j == 0)
    def _():
        m_scr[...] = jnp.full(m_scr.shape, NEG_INF, F32)
        l_scr[...] = jnp.zeros(l_scr.shape, F32)
        acc_scr[...] = jnp.zeros(acc_scr.shape, F32)

    def update(h, s, v_blocks):
        m_old = m_scr[h]
        m_new = jnp.maximum(m_old, jnp.max(s, axis=-1, keepdims=True))
        alpha = jnp.exp(m_old - m_new)
        p = jnp.exp(s - m_new)
        l_scr[h] = l_scr[h] * alpha + jnp.sum(p, axis=-1, keepdims=True)
        pv = None
        for r, vb in enumerate(v_blocks):
            term = _dot(p[:, r * PAGE_SIZE:(r + 1) * PAGE_SIZE].astype(BF16), vb)
            pv = term if pv is None else pv + term
        acc_scr[h] = acc_scr[h] * alpha + pv
        m_scr[h] = m_new

    q = q_ref[...]
    qbd = [_split_maps(q[:, h * E_A:(h + 1) * E_A]).astype(BF16) for h in range(H_A)]

    for h in range(H_A):
        cs = slice(h * E_A, (h + 1) * E_A)
        s = jnp.concatenate([_dot_nt(qbd[h], k_refs[r][:, cs].astype(BF16)) for r in range(pages)], axis=1)
        update(h, s, [v_refs[r][:, cs].astype(BF16) for r in range(pages)])

    @pl.when(j == pl.num_programs(1) - 1)
    def _():
        lam_val = _lambda_value(lam_ref) + lam_init
        row = lax.broadcasted_iota(I32, (2 * t, PAGE_SIZE), 0) % t
        col = lax.broadcasted_iota(I32, (2 * t, PAGE_SIZE), 1)
        visible = col <= row
        pad = jnp.zeros((PAGE_SIZE - t, E_A), F32)
        for h in range(H_A):
            cs = slice(h * E_A, (h + 1) * E_A)
            kn = jnp.concatenate([kn_ref[:, cs], pad], axis=0).astype(BF16)
            vn = jnp.concatenate([vn_ref[:, cs], pad], axis=0).astype(BF16)
            s = jnp.where(visible, _dot_nt(qbd[h], kn), NEG_INF)
            update(h, s, [vn])
            o = acc_scr[h] / l_scr[h]
            y = _diff_norm(o[:t], o[t:], lam_val, g_ref[...], lam_init)
            o_ref[:, cs] = y.astype(BF16).astype(o_ref.dtype)


def _attn_sample(q, kn, vn, cache_k, cache_v, page_table, lam, g_sub, batch, t, lam_init):
    pages = PAGES_PER_STEP
    n_pages = page_table.shape[1]
    steps = n_pages // pages
    fixed = lambda b, j, pt: (0, 0)
    rowb = lambda b, j, pt: (b, 0)

    def page_spec(r):
        return pl.BlockSpec((None, PAGE_SIZE, ATT_W), lambda b, j, pt: (pt[b, j * pages + r], 0, 0))

    grid_spec = pltpu.PrefetchScalarGridSpec(
        num_scalar_prefetch=1,
        grid=(batch, steps),
        in_specs=[pl.BlockSpec((t, ATT_W), rowb), pl.BlockSpec((t, ATT_W), rowb), pl.BlockSpec((t, ATT_W), rowb),
                  pl.BlockSpec((4, DH_A), fixed), pl.BlockSpec((1, E_A), fixed)]
                 + [page_spec(r) for r in range(pages)] + [page_spec(r) for r in range(pages)],
        out_specs=pl.BlockSpec((t, ATT_W), rowb),
        scratch_shapes=[pltpu.VMEM((H_A, 2 * t, 1), F32), pltpu.VMEM((H_A, 2 * t, 1), F32),
                        pltpu.VMEM((H_A, 2 * t, E_A), F32)],
    )
    return pl.pallas_call(
        functools.partial(_attn_sample_kernel, lam_init=lam_init, pages=pages),
        grid_spec=grid_spec,
        out_shape=jax.ShapeDtypeStruct(q.shape, F32),
        compiler_params=_cparams(2),
        name="attn_sample",
    )(page_table, q, kn, vn, lam, g_sub, *([cache_k] * pages), *([cache_v] * pages))


def _split3(x):
    a = x.astype(BF16)
    r = x - a.astype(F32)
    b = r.astype(BF16)
    c = (r - b.astype(F32)).astype(BF16)
    return a, b, c


def _hgrn_kernel(qr_ref, fl_ref, ir_ref, gr_ref, lb_ref, g_ref, s0_ref, y_ref, s_ref,
                 qe_scr, kd_scr, kk_scr, dl_scr, st_scr, *, chunk, blk):
    t = qr_ref.shape[0]
    lb = lb_ref[...]
    row = lax.broadcasted_iota(I32, (blk, blk), 0)
    col = lax.broadcasted_iota(I32, (blk, blk), 1)
    same = (row // chunk) == (col // chunk)
    cum_mask = jnp.where(jnp.logical_and(same, col <= row), 1.0, 0.0).astype(BF16)
    all_mask = jnp.where(same, 1.0, 0.0).astype(BF16)

    for b0 in range(0, t, blk):
        rs = slice(b0, b0 + blk)
        f = lb + (1.0 - lb) * jax.nn.sigmoid(fl_ref[rs, :])
        g = jnp.log(f)
        k = 1.0 - f
        g1, g2, g3 = _split3(g)
        bcum = _dot(cum_mask, g1) + _dot(cum_mask, g2) + _dot(cum_mask, g3)
        blast = _dot(all_mask, g1) + _dot(all_mask, g2) + _dot(all_mask, g3)
        qe_scr[rs, :] = qr_ref[rs, :] * jnp.exp(bcum)
        kd_scr[rs, :] = k * jnp.exp(-bcum)
        kk_scr[rs, :] = k * jnp.exp(blast - bcum)
        dl_scr[rs, :] = jnp.exp(blast)

    for h in range(H_R):
        st_scr[h] = s0_ref[h].T

    r_i = lax.broadcasted_iota(I32, (chunk, chunk), 0)
    c_i = lax.broadcasted_iota(I32, (chunk, chunk), 1)
    tril = c_i <= r_i
    g_rec = g_ref[...]

    def step(j, carry):
        r0 = pl.multiple_of(j * chunk, chunk)
        rows = pl.ds(r0, chunk)
        for h in range(H_R):
            cs = slice(h * K_R, (h + 1) * K_R)
            qe = qe_scr[rows, cs].astype(BF16)
            kd = kd_scr[rows, cs].astype(BF16)
            kk = kk_scr[rows, cs].astype(BF16)
            v = ir_ref[rows, cs]
            vb = v.astype(BF16)
            a = jnp.where(tril, _dot_nt(qe, kd), 0.0).astype(BF16)
            st = st_scr[h]
            o = _dot(a, vb) + _dot_nt(qe, st.astype(BF16))
            ut = _dot(v.T.astype(BF16), kk)
            st_scr[h] = st * dl_scr[pl.ds(r0, 1), cs] + ut
            y = _rms(o, g_rec) * _silu(gr_ref[rows, cs])
            y_ref[rows, cs] = y.astype(BF16).astype(y_ref.dtype)
        return carry

    lax.fori_loop(0, t // chunk, step, 0)
    for h in range(H_R):
        s_ref[h] = st_scr[h].T


def _hgrn(qr, fl, ir, gr, lb, g_rec, s0, batch, t, out_dtype):
    chunk = math.gcd(t, HGRN_CHUNK)
    blk = min(t, 256)
    rowb = lambda b: (b, 0)
    fixed = lambda b: (0, 0)
    seq = pl.BlockSpec((t, REC_W), rowb)
    state = pl.BlockSpec((None, H_R, K_R, V_R), lambda b: (b, 0, 0, 0))
    return pl.pallas_call(
        functools.partial(_hgrn_kernel, chunk=chunk, blk=blk),
        grid=(batch,),
        in_specs=[seq, seq, seq, seq, pl.BlockSpec((1, REC_W), fixed), pl.BlockSpec((1, V_R), fixed), state],
        out_specs=[seq, state],
        out_shape=[jax.ShapeDtypeStruct(qr.shape, out_dtype), jax.ShapeDtypeStruct(s0.shape, F32)],
        scratch_shapes=[pltpu.VMEM((t, REC_W), F32), pltpu.VMEM((t, REC_W), F32), pltpu.VMEM((t, REC_W), F32),
                        pltpu.VMEM((t, REC_W), F32), pltpu.VMEM((H_R, V_R, K_R), F32)],
        compiler_params=_cparams(1),
        name="hgrn",
    )(qr, fl, ir, gr, lb, g_rec, s0)


def _mix_kernel(x_ref, ya_ref, yr_ref, gmix_ref, wga_ref, wgr_ref, wba_ref, wbr_ref, wout_ref,
                gffn_ref, wrt_ref, brt_ref, xr_ref, cnt_ref, cnt_scr):
    i = pl.program_id(0)
    tm = x_ref.shape[0]

    @pl.when(i == 0)
    def _():
        cnt_scr[...] = jnp.zeros(cnt_scr.shape, F32)

    x = x_ref[...]
    h = _rms(x, gmix_ref[...]).astype(BF16)
    merged = (jax.nn.sigmoid(_dot(h, wga_ref[...])) * _dot(ya_ref[...], wba_ref[...])
              + jax.nn.sigmoid(_dot(h, wgr_ref[...])) * _dot(yr_ref[...], wbr_ref[...]))
    x1 = x + _dot(merged.astype(BF16), wout_ref[...])
    xr_ref[:, :D_MODEL] = x1

    hf = _rms(x1, gffn_ref[...]).astype(BF16)
    lg = _dot(hf, wrt_ref[...]) + brt_ref[...]
    lane = lax.broadcasted_iota(I32, (tm, LANES), 1)
    is_group = lane < N_GROUPS
    mg = jnp.max(jnp.where(is_group, lg, NEG_INF), axis=-1, keepdims=True)
    gidx = jnp.min(jnp.where(jnp.logical_and(is_group, lg == mg), lane, LANES), axis=-1, keepdims=True)
    p_top = 1.0 / jnp.sum(jnp.where(is_group, jnp.exp(lg - mg), 0.0), axis=-1, keepdims=True)
    base = N_GROUPS + EXPERTS_PER_GROUP * gidx
    in_grp = jnp.logical_and(lane >= base, lane < base + EXPERTS_PER_GROUP)
    v1 = jnp.max(jnp.where(in_grp, lg, NEG_INF), axis=-1, keepdims=True)
    e1 = jnp.min(jnp.where(jnp.logical_and(in_grp, lg == v1), lane, LANES), axis=-1, keepdims=True)
    rest = jnp.logical_and(in_grp, lane != e1)
    v2 = jnp.max(jnp.where(rest, lg, NEG_INF), axis=-1, keepdims=True)
    e2 = jnp.min(jnp.where(jnp.logical_and(rest, lg == v2), lane, LANES), axis=-1, keepdims=True)
    tt = jnp.exp(v2 - v1)
    w_a = (1.0 / (1.0 + tt)) * p_top
    w_b = (tt / (1.0 + tt)) * p_top
    a = e1 - base
    b = e2 - base
    a_first = a < b
    lo = jnp.minimum(a, b)
    hi = jnp.maximum(a, b)
    w_lo = jnp.where(a_first, w_a, w_b)
    w_hi = jnp.where(a_first, w_b, w_a)
    pair = jnp.where(lo == 0, hi - 1, jnp.where(lo == 1, hi + 1, 5))
    cls = gidx * N_PAIRS + pair

    onehot = lane == cls
    r_i = lax.broadcasted_iota(I32, (tm, tm), 0)
    c_i = lax.broadcasted_iota(I32, (tm, tm), 1)
    before = jnp.where(c_i < r_i, 1.0, 0.0).astype(BF16)
    excl = _dot(before, jnp.where(onehot, 1.0, 0.0).astype(BF16))
    rank = jnp.sum(jnp.where(onehot, excl + cnt_scr[...], 0.0), axis=-1, keepdims=True)
    cnt_scr[...] = cnt_scr[...] + jnp.sum(jnp.where(onehot, 1.0, 0.0), axis=0, keepdims=True)
    cnt_ref[...] = cnt_scr[...]

    route = jnp.where(lane == 0, cls.astype(F32),
                      jnp.where(lane == 1, w_lo, jnp.where(lane == 2, w_hi, jnp.where(lane == 3, rank, 0.0))))
    xr_ref[:, D_MODEL:] = route


def _mix(x, ya, yr, g_mix, wga, wgr, wba, wbr, wout, g_ffn, wrt, brt):
    n = x.shape[0]
    tm = TOKEN_TILE
    row = lambda i: (i, 0)
    fixed = lambda i: (0, 0)
    full = lambda a: pl.BlockSpec(a.shape, fixed)
    return pl.pallas_call(
        _mix_kernel,
        grid=(n // tm,),
        in_specs=[pl.BlockSpec((tm, D_MODEL), row), pl.BlockSpec((tm, ATT_W), row), pl.BlockSpec((tm, REC_W), row),
                  full(g_mix), full(wga), full(wgr), full(wba), full(wbr), full(wout), full(g_ffn), full(wrt), full(brt)],
        out_specs=[pl.BlockSpec((tm, ROW_W), row), pl.BlockSpec((1, LANES), fixed)],
        out_shape=[jax.ShapeDtypeStruct((n, ROW_W), F32), jax.ShapeDtypeStruct((1, LANES), F32)],
        scratch_shapes=[pltpu.VMEM((1, LANES), F32)],
        compiler_params=_cparams(1),
        name="mix",
    )(x, ya, yr, g_mix, wga, wgr, wba, wbr, wout, g_ffn, wrt, brt)


def _row_copy_out(x_ref, o_hbm, sem, r, p):
    return pltpu.make_async_copy(x_ref.at[pl.ds(r, 1)], o_hbm.at[pl.ds(p, 1)], sem)


def _dispatch_kernel(zs_ref, pos_ref, x_ref, o_hbm, zero_scr, zsem, rsem):
    i = pl.program_id(0)
    tm = x_ref.shape[0]

    def zero_copy(c):
        start = pl.multiple_of(jnp.maximum(zs_ref[c], 0), tm)
        return pltpu.make_async_copy(zero_scr, o_hbm.at[pl.ds(start, tm)], zsem)

    @pl.when(i == 0)
    def _():
        zero_scr[...] = jnp.zeros(zero_scr.shape, F32)
        for c in range(zs_ref.shape[0]):
            @pl.when(zs_ref[c] >= 0)
            def _():
                zero_copy(c).start()
        for c in range(zs_ref.shape[0]):
            @pl.when(zs_ref[c] >= 0)
            def _():
                zero_copy(c).wait()

    def issue(r, carry):
        _row_copy_out(x_ref, o_hbm, rsem, r, pos_ref[0, r]).start()
        return carry

    lax.fori_loop(0, tm, issue, 0)

    def drain(r, carry):
        _row_copy_out(x_ref, o_hbm, rsem, r, pos_ref[0, r]).wait()
        return carry

    lax.fori_loop(0, tm, drain, 0)


def _dispatch(xr, pos2d, zero_starts, n_rows_sorted):
    n = xr.shape[0]
    tm = TOKEN_TILE
    grid_spec = pltpu.PrefetchScalarGridSpec(
        num_scalar_prefetch=1,
        grid=(n // tm,),
        in_specs=[pl.BlockSpec((None, 1, tm), lambda i, zs: (i, 0, 0), memory_space=pltpu.SMEM),
                  pl.BlockSpec((tm, ROW_W), lambda i, zs: (i, 0))],
        out_specs=pl.BlockSpec(memory_space=pl.ANY),
        scratch_shapes=[pltpu.VMEM((tm, ROW_W), F32), pltpu.SemaphoreType.DMA(()), pltpu.SemaphoreType.DMA(())],
    )
    return pl.pallas_call(
        _dispatch_kernel,
        grid_spec=grid_spec,
        out_shape=jax.ShapeDtypeStruct((n_rows_sorted, ROW_W), F32),
        compiler_params=_cparams(1),
        name="dispatch",
    )(zero_starts, pos2d, xr)


def _moe_kernel(blk_ref, elo_ref, ehi_ref, valid_ref, xs_ref, g_ref,
                wg_lo_ref, wg_hi_ref, wu_lo_ref, wu_hi_ref, wd_lo_ref, wd_hi_ref, o_ref):
    j = pl.program_id(0)

    @pl.when(valid_ref[j] == 1)
    def _():
        x = xs_ref[:, :D_MODEL]
        w_lo = xs_ref[:, D_MODEL + 1:D_MODEL + 2]
        w_hi = xs_ref[:, D_MODEL + 2:D_MODEL + 3]
        h = _rms(x, g_ref[...]).astype(BF16)
        hid_lo = (_silu(_dot(h, wg_lo_ref[...])) * _dot(h, wu_lo_ref[...]) * w_lo).astype(BF16)
        hid_hi = (_silu(_dot(h, wg_hi_ref[...])) * _dot(h, wu_hi_ref[...]) * w_hi).astype(BF16)
        o_ref[...] = x + (_dot(hid_lo, wd_lo_ref[...]) + _dot(hid_hi, wd_hi_ref[...]))

    @pl.when(valid_ref[j] == 0)
    def _():
        o_ref[...] = jnp.zeros(o_ref.shape, F32)


def _moe(xs, g_ffn, wg, wu, wd, in_blk, e_lo, e_hi, valid):
    tm = TOKEN_TILE
    n_tiles = xs.shape[0] // tm
    lo = lambda j, blk, elo, ehi, va: (elo[j], 0, 0)
    hi = lambda j, blk, elo, ehi, va: (ehi[j], 0, 0)
    up = pl.BlockSpec((None, D_MODEL, D_EXPERT), lo), pl.BlockSpec((None, D_MODEL, D_EXPERT), hi)
    down = pl.BlockSpec((None, D_EXPERT, D_MODEL), lo), pl.BlockSpec((None, D_EXPERT, D_MODEL), hi)
    grid_spec = pltpu.PrefetchScalarGridSpec(
        num_scalar_prefetch=4,
        grid=(n_tiles,),
        in_specs=[pl.BlockSpec((tm, ROW_W), lambda j, blk, elo, ehi, va: (blk[j], 0)),
                  pl.BlockSpec((1, D_MODEL), lambda j, blk, elo, ehi, va: (0, 0)),
                  up[0], up[1], up[0], up[1], down[0], down[1]],
        out_specs=pl.BlockSpec((tm, D_MODEL), lambda j, blk, elo, ehi, va: (j, 0)),
    )
    return pl.pallas_call(
        _moe_kernel,
        grid_spec=grid_spec,
        out_shape=jax.ShapeDtypeStruct((xs.shape[0], D_MODEL), F32),
        compiler_params=_cparams(1),
        name="moe",
    )(in_blk, e_lo, e_hi, valid, xs, g_ffn, wg, wg, wu, wu, wd, wd)


def _row_copy_in(x_hbm, buf, sem, r, p):
    return pltpu.make_async_copy(x_hbm.at[pl.ds(p, 1)], buf.at[pl.ds(r, 1)], sem)


def _ple_kernel(pos_ref, xs_hbm, p_ref, gple_ref, wpg_ref, wp_ref, gfin_ref, y_ref, buf, sem):
    tm = buf.shape[0]

    def issue(r, carry):
        _row_copy_in(xs_hbm, buf, sem, r, pos_ref[0, r]).start()
        return carry

    lax.fori_loop(0, tm, issue, 0)

    def drain(r, carry):
        _row_copy_in(xs_hbm, buf, sem, r, pos_ref[0, r]).wait()
        return carry

    lax.fori_loop(0, tm, drain, 0)

    x2 = buf[...]
    hp = _rms(x2, gple_ref[...]).astype(BF16)
    gate = jax.nn.sigmoid(_dot(hp, wpg_ref[...]))
    x3 = x2 + gate * _dot(p_ref[...].astype(BF16), wp_ref[...])
    y_ref[...] = _rms(x3, gfin_ref[...])


def _ple(xs2, pos2d, p, g_ple, wpg, wp, g_final):
    n = p.shape[0]
    tm = TOKEN_TILE
    row = lambda i: (i, 0)
    fixed = lambda i: (0, 0)
    return pl.pallas_call(
        _ple_kernel,
        grid=(n // tm,),
        in_specs=[pl.BlockSpec((None, 1, tm), lambda i: (i, 0, 0), memory_space=pltpu.SMEM),
                  pl.BlockSpec(memory_space=pl.ANY),
                  pl.BlockSpec((tm, P_DIM), row), pl.BlockSpec((1, D_MODEL), fixed),
                  pl.BlockSpec((D_MODEL, D_MODEL), fixed), pl.BlockSpec((P_DIM, D_MODEL), fixed),
                  pl.BlockSpec((1, D_MODEL), fixed)],
        out_specs=pl.BlockSpec((tm, D_MODEL), row),
        out_shape=jax.ShapeDtypeStruct((n, D_MODEL), F32),
        scratch_shapes=[pltpu.VMEM((tm, D_MODEL), F32), pltpu.SemaphoreType.DMA(())],
        compiler_params=_cparams(1),
        name="ple",
    )(pos2d, xs2, p, g_ple, wpg, wp, g_final)


def _routing_plan(xr, counts, n):
    tm = TOKEN_TILE
    n_tiles = n // tm + N_CLASSES
    cls = xr[:, D_MODEL].astype(I32)
    rank = xr[:, D_MODEL + 3].astype(I32)
    cnt = counts[0, :N_CLASSES].astype(I32)
    tiles = (cnt + tm - 1) // tm
    tile_end = jnp.cumsum(tiles)
    tile_start = tile_end - tiles
    n_used = tile_end[-1]
    pos = (tile_start * tm)[cls] + rank
    j = jnp.arange(n_tiles, dtype=I32)
    valid = j < n_used
    in_blk = jnp.minimum(j, n_used - 1)
    tcls = jnp.searchsorted(tile_end, in_blk, side="right").astype(I32)
    grp = tcls // N_PAIRS
    pair = tcls % N_PAIRS
    e_lo = grp * EXPERTS_PER_GROUP + jnp.asarray(PAIR_LO, I32)[pair]
    e_hi = grp * EXPERTS_PER_GROUP + jnp.asarray(PAIR_HI, I32)[pair]
    seg_zero = jnp.where(tiles > 0, tile_end * tm - tm, -1)
    tail = n_used + jnp.arange(N_CLASSES, dtype=I32)
    tail_zero = jnp.where(tail < n_tiles, tail * tm, -1)
    zero_starts = jnp.concatenate([seg_zero, tail_zero]).astype(I32)
    return pos.reshape(n // tm, 1, tm), zero_starts, in_blk, e_lo, e_hi, valid.astype(I32), n_tiles * tm


def _rope_tables(pos, rows):
    half = DH_A // 2
    inv_freq = ROPE_THETA ** (-jnp.arange(half, dtype=F32) / half)
    ang = pos.astype(F32)[:, None] * inv_freq[None, :]
    cos = jnp.cos(ang)
    sin = jnp.sin(ang)
    cos_t = jnp.tile(jnp.concatenate([cos, cos], axis=-1), (1, ATT_W // DH_A))
    sin_t = jnp.tile(jnp.concatenate([-sin, sin], axis=-1), (1, ATT_W // DH_A))
    reps = max(rows // pos.shape[0], 1)
    return jnp.tile(cos_t, (reps, 1)), jnp.tile(sin_t, (reps, 1))


def _layer(i, x, p, pos, attend, s0, w, batch, seq):
    n = batch * seq
    small = BF16 if seq % 16 == 0 else F32
    cos_t, sin_t = _rope_tables(pos, TOKEN_TILE)
    q, kf, kb, vf, vb, qr, fl, ir, gr = _inproj(x, w["g_mix"], w["w_seq"], cos_t, sin_t, seq, small)
    lam_init = 0.8 - 0.6 * math.exp(-0.3 * i)
    ya = attend(q, kf, vf, kb, vb, lam_init)
    yr, s_new = _hgrn(qr, fl, ir, gr, w["lb"], w["g_rec"], s0, batch, seq, small)
    xr, counts = _mix(x, ya.astype(BF16), yr.astype(BF16), w["g_mix"], w["w_gate_a"], w["w_gate_r"], w["w_branch_a"],
                      w["w_branch_r"], w["w_out"], w["g_ffn"], w["w_route"], w["b_route"])
    pos2d, zero_starts, in_blk, e_lo, e_hi, valid, n_sorted = _routing_plan(xr, counts, n)
    xs = _dispatch(xr, pos2d, zero_starts, n_sorted)
    xs2 = _moe(xs, w["g_ffn"], w["w_exp_gate"], w["w_exp_up"], w["w_exp_down"], in_blk, e_lo, e_hi, valid)
    y = _ple(xs2, pos2d, p, w["g_ple"], w["w_ple_gate"], w["w_ple"], w["g_final"])
    return y, kf, vf, s_new


def kernel(x_prompt, x_sample, p_prompt, p_sample, cache_k, cache_v, state_hgrn, page_table, g_mix, w_in, lam,
           g_subln, lb_param, g_rec, w_branch_a, w_branch_r, w_out, g_ffn, w_route_group, b_route_group,
           w_route_expert, b_route_expert, w_exp_gate, w_exp_up, w_exp_down, g_ple, w_ple_gate, w_ple, g_final):
    depth = w_in.shape[0]
    assert depth == 1, "single-layer step"
    bp, sp, _ = x_prompt.shape
    bs, ts, _ = x_sample.shape
    past_len = page_table.shape[1] * PAGE_SIZE
    i = 0

    w_in_b = w_in[i].astype(BF16)
    n_route = N_GROUPS + N_EXPERTS
    w_route = jnp.concatenate([w_route_group[i], w_route_expert[i]], axis=1)
    w_route = jnp.pad(w_route, ((0, 0), (0, LANES - n_route))).astype(BF16)
    b_route = jnp.pad(jnp.concatenate([b_route_group[i], b_route_expert[i]]), (0, LANES - n_route)).reshape(1, LANES)
    lb = jnp.cumsum(jax.nn.softmax(lb_param.astype(F32), axis=0), axis=0)[i].reshape(1, REC_W)
    w = dict(
        g_mix=g_mix[i].reshape(1, D_MODEL), w_seq=w_in_b[:, :N_SEQ_COLS],
        w_gate_a=w_in_b[:, N_SEQ_COLS:N_SEQ_COLS + D_MODEL], w_gate_r=w_in_b[:, N_SEQ_COLS + D_MODEL:],
        lb=lb, g_rec=g_rec[i].reshape(1, V_R),
        w_branch_a=w_branch_a[i].astype(BF16), w_branch_r=w_branch_r[i].astype(BF16), w_out=w_out[i].astype(BF16),
        g_ffn=g_ffn[i].reshape(1, D_MODEL), w_route=w_route, b_route=b_route,
        w_exp_gate=w_exp_gate[i].astype(BF16), w_exp_up=w_exp_up[i].astype(BF16), w_exp_down=w_exp_down[i].astype(BF16),
        g_ple=g_ple[i].reshape(1, D_MODEL), w_ple_gate=w_ple_gate[i].astype(BF16), w_ple=w_ple[i].astype(BF16),
        g_final=g_final.reshape(1, D_MODEL),
    )
    lam_i = lam[i].astype(F32)
    g_sub = g_subln[i].reshape(1, E_A)

    def attend_prompt(q, kf, vf, kb, vb, lam_init):
        return _attn_prompt(q, kb, vb, lam_i, g_sub, bp, sp, lam_init)

    ck = cache_k[i].reshape(cache_k.shape[1], PAGE_SIZE, ATT_W)
    cv = cache_v[i].reshape(cache_v.shape[1], PAGE_SIZE, ATT_W)

    def attend_sample(q, kf, vf, kb, vb, lam_init):
        return _attn_sample(q, kf, vf, ck, cv, page_table, lam_i, g_sub, bs, ts, lam_init)

    pos_p = jnp.arange(sp, dtype=I32)
    pos_s = past_len + jnp.arange(ts, dtype=I32)
    s0_p = jnp.zeros((bp, H_R, K_R, V_R), F32)

    y_p, k_p, v_p, s_p = _layer(i, x_prompt.reshape(bp * sp, D_MODEL), p_prompt[i].reshape(bp * sp, P_DIM),
                                pos_p, attend_prompt, s0_p, w, bp, sp)
    y_s, k_s, v_s, s_s = _layer(i, x_sample.reshape(bs * ts, D_MODEL), p_sample[i].reshape(bs * ts, P_DIM),
                                pos_s, attend_sample, state_hgrn[i], w, bs, ts)

    return (y_p.reshape(bp, sp, D_MODEL), y_s.reshape(bs, ts, D_MODEL),
            k_p.reshape(1, bp, sp, H_A, 2, DH_A), v_p.reshape(1, bp, sp, H_A, E_A), s_p.reshape(1, bp, H_R, K_R, V_R),
            k_s.reshape(1, bs, ts, H_A, 2, DH_A), v_s.reshape(1, bs, ts, H_A, E_A), s_s.reshape(1, bs, H_R, K_R, V_R))
```

```python
import functools
import math

import jax
import jax.numpy as jnp
import numpy as np
from jax import lax
from jax.experimental import pallas as pl
from jax.experimental.pallas import tpu as pltpu

F32 = jnp.float32
BF16 = jnp.bfloat16
I32 = jnp.int32

D_MODEL = 1024
P_DIM = 256
H_A = 4
DH_A = 64
E_A = 2 * DH_A
H_R = 4
K_R = 128
V_R = 128
HGRN_CHUNK = 32
N_GROUPS = 4
EXPERTS_PER_GROUP = 4
N_EXPERTS = N_GROUPS * EXPERTS_PER_GROUP
D_EXPERT = 256
ROPE_THETA = 10000.0
EPS = 1e-6
NEG_INF = -1e30
PAGE_SIZE = 128
ATT_W = H_A * E_A
REC_W = H_R * V_R
N_SEQ_COLS = 7 * 512
LANES = 128
N_PAIRS = 6
N_CLASSES = N_GROUPS * N_PAIRS
PAIR_LO = (0, 0, 0, 1, 1, 2)
PAIR_HI = (1, 2, 3, 2, 3, 3)
ROW_W = D_MODEL + LANES
TOKEN_TILE = 256
ATTN_TILE = 512
PAGES_PER_STEP = 8
VMEM_LIMIT = 56 * 1024 * 1024


def _cparams(n_axes):
    return pltpu.CompilerParams(dimension_semantics=("arbitrary",) * n_axes, vmem_limit_bytes=VMEM_LIMIT)


def _rms(x, g):
    return x * lax.rsqrt(jnp.mean(x * x, axis=-1, keepdims=True) + EPS) * g


def _silu(x):
    return x * jax.nn.sigmoid(x)


def _dot(a, b):
    return jnp.dot(a, b, preferred_element_type=F32)


def _dot_nt(a, b):
    return lax.dot_general(a, b, (((1,), (1,)), ((), ())), preferred_element_type=F32)


def _inproj_kernel(x_ref, g_ref, w_ref, cos_ref, sin_ref, *out_refs, transposed):
    h = _rms(x_ref[...], g_ref[...]).astype(BF16)
    cos = cos_ref[...]
    sin = sin_ref[...]
    tm = h.shape[0]
    lane = lax.broadcasted_iota(I32, (tm, ATT_W), 1)
    first_half = (lane % DH_A) < (DH_A // 2)

    def proj(c):
        return _dot(h, w_ref[:, c * 512:(c + 1) * 512])

    def rope(t):
        rot = jnp.where(first_half, pltpu.roll(t, ATT_W - DH_A // 2, 1), pltpu.roll(t, DH_A // 2, 1))
        return t * cos + rot * sin

    q = rope(proj(0)) * (DH_A ** -0.5)
    k = rope(proj(1))
    v = proj(2)
    if transposed:
        qt_ref, kt_ref, kb_ref, vf_ref, vt_ref = out_refs[:5]
        qt_ref[...] = q.T.astype(BF16)
        kt_ref[...] = k.T
        kb_ref[...] = k.astype(BF16)
        vf_ref[...] = v
        vt_ref[...] = v.T.astype(BF16)
        rest = out_refs[5:]
    else:
        q_ref, kf_ref, vf_ref = out_refs[:3]
        q_ref[...] = q
        kf_ref[...] = k
        vf_ref[...] = v
        rest = out_refs[3:]
    qr_ref, fl_ref, ir_ref, gr_ref = rest
    qr_ref[...] = _silu(proj(3))
    fl_ref[...] = proj(4)
    ir_ref[...] = proj(5)
    gr_ref[...] = proj(6)


def _inproj(x, g_mix, w_seq, cos_t, sin_t, batch, seq, transposed):
    n = x.shape[0]
    tm = TOKEN_TILE
    nblk = max(seq // tm, 1)
    row = lambda i: (i, 0)
    tab = lambda i: (i % nblk, 0)
    const = lambda i: (0, 0)
    o512 = pl.BlockSpec((tm, 512), row)
    sds = lambda dt: jax.ShapeDtypeStruct((n, 512), dt)
    if transposed:
        ot = pl.BlockSpec((None, 512, tm), lambda i: (i // nblk, 0, i % nblk))
        sdt = lambda dt: jax.ShapeDtypeStruct((batch, 512, seq), dt)
        out_specs = [ot, ot, o512, o512, ot] + [o512] * 4
        out_shape = [sdt(BF16), sdt(F32), sds(BF16), sds(F32), sdt(BF16)] + [sds(F32)] * 4
    else:
        out_specs = [o512] * 7
        out_shape = [sds(F32)] * 7
    return pl.pallas_call(
        functools.partial(_inproj_kernel, transposed=transposed),
        grid=(n // tm,),
        in_specs=[pl.BlockSpec((tm, D_MODEL), row), pl.BlockSpec((1, D_MODEL), const),
                  pl.BlockSpec((D_MODEL, N_SEQ_COLS), const),
                  pl.BlockSpec((tm, ATT_W), tab), pl.BlockSpec((tm, ATT_W), tab)],
        out_specs=out_specs,
        out_shape=out_shape,
        compiler_params=_cparams(1),
        name="inproj",
    )(x, g_mix, w_seq, cos_t, sin_t)


def _lambda_value(lam_ref):
    lam = lam_ref[...]
    s01 = jnp.sum(lam[0:1, :] * lam[1:2, :], axis=-1, keepdims=True)
    s23 = jnp.sum(lam[2:3, :] * lam[3:4, :], axis=-1, keepdims=True)
    return jnp.exp(s01) - jnp.exp(s23)


def _diff_norm(o0, o1, lam_val, g_sub, lam_init):
    od = o0 - lam_val * o1
    return _rms(od, g_sub) * (1.0 - lam_init)


def _split_maps(qh):
    lane = lax.broadcasted_iota(I32, qh.shape, 1)
    zero = jnp.zeros_like(qh)
    return jnp.concatenate([jnp.where(lane < DH_A, qh, zero), jnp.where(lane >= DH_A, qh, zero)], axis=0)


def _attn_prompt_kernel(q_ref, k_ref, v_ref, lam_ref, g_ref, o_ref, m_scr, l_scr, acc_scr, *, lam_init):
    i = pl.program_id(1)
    tq = q_ref.shape[1]
    lam_val = _lambda_value(lam_ref) + lam_init
    key = lax.broadcasted_iota(I32, (tq, 2 * tq), 0)
    qry = lax.broadcasted_iota(I32, (tq, 2 * tq), 1) % tq
    hidden = key > qry
    feat = lax.broadcasted_iota(I32, (E_A, tq), 0)
    for h in range(H_A):
        cs = slice(h * E_A, (h + 1) * E_A)
        qh = q_ref[cs, :].astype(F32)
        q2 = jnp.concatenate([jnp.where(feat < DH_A, qh, 0.0), jnp.where(feat >= DH_A, qh, 0.0)],
                             axis=1).astype(BF16)
        m_scr[...] = jnp.full(m_scr.shape, NEG_INF, F32)
        l_scr[...] = jnp.zeros(l_scr.shape, F32)
        acc_scr[...] = jnp.zeros(acc_scr.shape, F32)

        def body(kb, carry):
            r0 = pl.multiple_of(kb * tq, tq)
            s = _dot(k_ref[pl.ds(r0, tq), cs], q2)
            s = jnp.where(jnp.logical_and(kb == i, hidden), NEG_INF, s)
            m_old = m_scr[...]
            m_new = jnp.maximum(m_old, jnp.max(s, axis=0, keepdims=True))
            alpha = jnp.exp(m_old - m_new)
            p = jnp.exp(s - m_new)
            l_scr[...] = l_scr[...] * alpha + jnp.sum(p, axis=0, keepdims=True)
            acc_scr[...] = acc_scr[...] * alpha + _dot(v_ref[cs, pl.ds(r0, tq)], p.astype(BF16))
            m_scr[...] = m_new
            return carry

        lax.fori_loop(0, i + 1, body, 0)
        o = acc_scr[...] / l_scr[...]
        od = o[:, :tq] - lam_val * o[:, tq:]
        ms = jnp.mean(od * od, axis=0, keepdims=True)
        y_t = od * lax.rsqrt(ms + EPS) * g_ref[...] * (1.0 - lam_init)
        o_ref[:, cs] = y_t.T.astype(o_ref.dtype)


def _attn_prompt(q_t, k, v_t, lam, g_sub_col, batch, seq, lam_init):
    tq = min(ATTN_TILE, seq)
    nq = seq // tq
    return pl.pallas_call(
        functools.partial(_attn_prompt_kernel, lam_init=lam_init),
        grid=(batch, nq),
        in_specs=[pl.BlockSpec((None, ATT_W, tq), lambda b, i: (b, 0, i)),
                  pl.BlockSpec((seq, ATT_W), lambda b, i: (b, 0)),
                  pl.BlockSpec((None, ATT_W, seq), lambda b, i: (b, 0, 0)),
                  pl.BlockSpec((4, DH_A), lambda b, i: (0, 0)),
                  pl.BlockSpec((E_A, 1), lambda b, i: (0, 0))],
        out_specs=pl.BlockSpec((tq, ATT_W), lambda b, i: (b * nq + i, 0)),
        out_shape=jax.ShapeDtypeStruct(k.shape, BF16),
        scratch_shapes=[pltpu.VMEM((1, 2 * tq), F32), pltpu.VMEM((1, 2 * tq), F32),
                        pltpu.VMEM((E_A, 2 * tq), F32)],
        compiler_params=_cparams(2),
        name="attn_prompt",
    )(q_t, k, v_t, lam, g_sub_col)


def _attn_sample_kernel(pt_ref, q_ref, kn_ref, vn_ref, lam_ref, g_ref, *refs, lam_init, pages):
    k_refs = refs[:pages]
    v_refs = refs[pages:2 * pages]
    o_ref = refs[2 * pages]
    m_scr, l_scr, acc_scr = refs[2 * pages + 1:]
    j = pl.program_id(1)
    t = q_ref.shape[0]

    @pl.when(j == 0)
    def _():
        m_scr[...] = jnp.full(m_scr.shape, NEG_INF, F32)
        l_scr[...] = jnp.zeros(l_scr.shape, F32)
        acc_scr[...] = jnp.zeros(acc_scr.shape, F32)

    rows_h = 2 * t

    def own_head_blocks(big):
        return jnp.concatenate([big[h * rows_h:(h + 1) * rows_h, h * E_A:(h + 1) * E_A] for h in range(H_A)], axis=0)

    def update(s, v_all):
        m_old = m_scr[...]
        m_new = jnp.maximum(m_old, jnp.max(s, axis=-1, keepdims=True))
        alpha = jnp.exp(m_old - m_new)
        p = jnp.exp(s - m_new)
        l_scr[...] = l_scr[...] * alpha + jnp.sum(p, axis=-1, keepdims=True)
        acc_scr[...] = acc_scr[...] * alpha + own_head_blocks(_dot(p.astype(BF16), v_all))
        m_scr[...] = m_new

    q = q_ref[...]
    zero = jnp.zeros((rows_h, E_A), F32)
    qall = jnp.concatenate(
        [jnp.concatenate([_split_maps(q[:, h * E_A:(h + 1) * E_A]) if hh == h else zero for hh in range(H_A)], axis=1)
         for h in range(H_A)], axis=0).astype(BF16)

    kt = jnp.concatenate([k_refs[r][...].astype(BF16) for r in range(pages)], axis=1)
    v_all = jnp.concatenate(
        [jnp.concatenate([v_refs[r][pl.ds(h, PAGE_SIZE, stride=H_A), :] for h in range(H_A)], axis=1)
         for r in range(pages)], axis=0).astype(BF16)
    update(_dot(qall, kt), v_all)

    @pl.when(j == pl.num_programs(1) - 1)
    def _():
        lam_val = _lambda_value(lam_ref) + lam_init
        n_rows = H_A * rows_h
        row_t = lax.broadcasted_iota(I32, (n_rows, PAGE_SIZE), 0) % t
        col = lax.broadcasted_iota(I32, (n_rows, PAGE_SIZE), 1)
        visible = col <= row_t
        pad = jnp.zeros((PAGE_SIZE - t, ATT_W), F32)
        kn_t = jnp.concatenate([kn_ref[...], pad], axis=0).T.astype(BF16)
        vn = jnp.concatenate([vn_ref[...], pad], axis=0).astype(BF16)
        update(jnp.where(visible, _dot(qall, kn_t), NEG_INF), vn)
        o = acc_scr[...] / l_scr[...]
        for h in range(H_A):
            o0 = o[h * rows_h:h * rows_h + t]
            o1 = o[h * rows_h + t:(h + 1) * rows_h]
            y = _diff_norm(o0, o1, lam_val, g_ref[...], lam_init)
            o_ref[:, h * E_A:(h + 1) * E_A] = y.astype(BF16).astype(o_ref.dtype)


def _attn_sample(q, kn, vn, cache_kt, cache_v2, page_table, lam, g_sub, batch, t, lam_init):
    pages = PAGES_PER_STEP
    n_pages = page_table.shape[1]
    steps = n_pages // pages
    fixed = lambda b, j, pt: (0, 0)
    rowb = lambda b, j, pt: (b, 0)
    n_rows = H_A * 2 * t

    def page_spec(r):
        return pl.BlockSpec((None, ATT_W, PAGE_SIZE), lambda b, j, pt: (pt[b, j * pages + r], 0, 0))

    grid_spec = pltpu.PrefetchScalarGridSpec(
        num_scalar_prefetch=1,
        grid=(batch, steps),
        in_specs=[pl.BlockSpec((t, ATT_W), rowb), pl.BlockSpec((t, ATT_W), rowb), pl.BlockSpec((t, ATT_W), rowb),
                  pl.BlockSpec((4, DH_A), fixed), pl.BlockSpec((1, E_A), fixed)]
                 + [page_spec(r) for r in range(pages)] + [page_spec(r) for r in range(pages)],
        out_specs=pl.BlockSpec((t, ATT_W), rowb),
        scratch_shapes=[pltpu.VMEM((n_rows, 1), F32), pltpu.VMEM((n_rows, 1), F32), pltpu.VMEM((n_rows, E_A), F32)],
    )
    return pl.pallas_call(
        functools.partial(_attn_sample_kernel, lam_init=lam_init, pages=pages),
        grid_spec=grid_spec,
        out_shape=jax.ShapeDtypeStruct(q.shape, F32),
        compiler_params=_cparams(2),
        name="attn_sample",
    )(page_table, q, kn, vn, lam, g_sub, *([cache_kt] * pages), *([cache_v2] * pages))


def _split3(x):
    a = x.astype(BF16)
    r = x - a.astype(F32)
    b = r.astype(BF16)
    c = (r - b.astype(F32)).astype(BF16)
    return a, b, c


def _hgrn_kernel(qr_ref, fl_ref, ir_ref, gr_ref, lb_ref, g_ref, s0_ref, y_ref, s_ref,
                 qe_scr, kd_scr, kk_scr, dl_scr, st_scr, *, chunk, blk):
    t = qr_ref.shape[0]
    lb = lb_ref[...]
    row = lax.broadcasted_iota(I32, (blk, blk), 0)
    col = lax.broadcasted_iota(I32, (blk, blk), 1)
    same = (row // chunk) == (col // chunk)
    cum_mask = jnp.where(jnp.logical_and(same, col <= row), 1.0, 0.0).astype(BF16)
    all_mask = jnp.where(same, 1.0, 0.0).astype(BF16)

    for b0 in range(0, t, blk):
        rs = slice(b0, b0 + blk)
        f = lb + (1.0 - lb) * jax.nn.sigmoid(fl_ref[rs, :])
        g = jnp.log(f)
        k = 1.0 - f
        g1, g2, g3 = _split3(g)
        bcum = _dot(cum_mask, g1) + _dot(cum_mask, g2) + _dot(cum_mask, g3)
        blast = _dot(all_mask, g1) + _dot(all_mask, g2) + _dot(all_mask, g3)
        qe_scr[rs, :] = qr_ref[rs, :] * jnp.exp(bcum)
        kd_scr[rs, :] = k * jnp.exp(-bcum)
        kk_scr[rs, :] = k * jnp.exp(blast - bcum)
        dl_scr[rs, :] = jnp.exp(blast)

    for h in range(H_R):
        st_scr[h] = s0_ref[h].T

    r_i = lax.broadcasted_iota(I32, (chunk, chunk), 0)
    c_i = lax.broadcasted_iota(I32, (chunk, chunk), 1)
    tril = c_i <= r_i
    g_rec = g_ref[...]

    def step(j, carry):
        r0 = pl.multiple_of(j * chunk, chunk)
        rows = pl.ds(r0, chunk)
        for h in range(H_R):
            cs = slice(h * K_R, (h + 1) * K_R)
            qe = qe_scr[rows, cs].astype(BF16)
            kd = kd_scr[rows, cs].astype(BF16)
            kk = kk_scr[rows, cs].astype(BF16)
            v = ir_ref[rows, cs]
            vb = v.astype(BF16)
            a = jnp.where(tril, _dot_nt(qe, kd), 0.0).astype(BF16)
            st = st_scr[h]
            o = _dot(a, vb) + _dot_nt(qe, st.astype(BF16))
            ut = _dot(v.T.astype(BF16), kk)
            st_scr[h] = st * dl_scr[pl.ds(r0, 1), cs] + ut
            y = _rms(o, g_rec) * _silu(gr_ref[rows, cs])
            y_ref[rows, cs] = y.astype(BF16).astype(y_ref.dtype)
        return carry

    lax.fori_loop(0, t // chunk, step, 0)
    for h in range(H_R):
        s_ref[h] = st_scr[h].T


def _hgrn(qr, fl, ir, gr, lb, g_rec, s0, batch, t, out_dtype):
    chunk = math.gcd(t, HGRN_CHUNK)
    blk = min(t, 256)
    rowb = lambda b: (b, 0)
    fixed = lambda b: (0, 0)
    seq = pl.BlockSpec((t, REC_W), rowb)
    state = pl.BlockSpec((None, H_R, K_R, V_R), lambda b: (b, 0, 0, 0))
    return pl.pallas_call(
        functools.partial(_hgrn_kernel, chunk=chunk, blk=blk),
        grid=(batch,),
        in_specs=[seq, seq, seq, seq, pl.BlockSpec((1, REC_W), fixed), pl.BlockSpec((1, V_R), fixed), state],
        out_specs=[seq, state],
        out_shape=[jax.ShapeDtypeStruct(qr.shape, out_dtype), jax.ShapeDtypeStruct(s0.shape, F32)],
        scratch_shapes=[pltpu.VMEM((t, REC_W), F32), pltpu.VMEM((t, REC_W), F32), pltpu.VMEM((t, REC_W), F32),
                        pltpu.VMEM((t, REC_W), F32), pltpu.VMEM((H_R, V_R, K_R), F32)],
        compiler_params=_cparams(1),
        name="hgrn",
    )(qr, fl, ir, gr, lb, g_rec, s0)


def _mix_kernel(x_ref, ya_ref, yr_ref, gmix_ref, wga_ref, wgr_ref, wba_ref, wbr_ref, wout_ref,
                gffn_ref, wrt_ref, brt_ref, xr_ref, cnt_ref, cnt_scr):
    i = pl.program_id(0)
    tm = x_ref.shape[0]

    @pl.when(i == 0)
    def _():
        cnt_scr[...] = jnp.zeros(cnt_scr.shape, F32)

    x = x_ref[...]
    h = _rms(x, gmix_ref[...]).astype(BF16)
    merged = (jax.nn.sigmoid(_dot(h, wga_ref[...])) * _dot(ya_ref[...], wba_ref[...])
              + jax.nn.sigmoid(_dot(h, wgr_ref[...])) * _dot(yr_ref[...], wbr_ref[...]))
    x1 = x + _dot(merged.astype(BF16), wout_ref[...])
    xr_ref[:, :D_MODEL] = x1

    hf = _rms(x1, gffn_ref[...]).astype(BF16)
    lg = _dot(hf, wrt_ref[...]) + brt_ref[...]
    lane = lax.broadcasted_iota(I32, (tm, LANES), 1)
    is_group = lane < N_GROUPS
    mg = jnp.max(jnp.where(is_group, lg, NEG_INF), axis=-1, keepdims=True)
    gidx = jnp.min(jnp.where(jnp.logical_and(is_group, lg == mg), lane, LANES), axis=-1, keepdims=True)
    p_top = 1.0 / jnp.sum(jnp.where(is_group, jnp.exp(lg - mg), 0.0), axis=-1, keepdims=True)
    base = N_GROUPS + EXPERTS_PER_GROUP * gidx
    in_grp = jnp.logical_and(lane >= base, lane < base + EXPERTS_PER_GROUP)
    v1 = jnp.max(jnp.where(in_grp, lg, NEG_INF), axis=-1, keepdims=True)
    e1 = jnp.min(jnp.where(jnp.logical_and(in_grp, lg == v1), lane, LANES), axis=-1, keepdims=True)
    rest = jnp.logical_and(in_grp, lane != e1)
    v2 = jnp.max(jnp.where(rest, lg, NEG_INF), axis=-1, keepdims=True)
    e2 = jnp.min(jnp.where(jnp.logical_and(rest, lg == v2), lane, LANES), axis=-1, keepdims=True)
    tt = jnp.exp(v2 - v1)
    w_a = (1.0 / (1.0 + tt)) * p_top
    w_b = (tt / (1.0 + tt)) * p_top
    a = e1 - base
    b = e2 - base
    a_first = a < b
    lo = jnp.minimum(a, b)
    hi = jnp.maximum(a, b)
    w_lo = jnp.where(a_first, w_a, w_b)
    w_hi = jnp.where(a_first, w_b, w_a)
    pair = jnp.where(lo == 0, hi - 1, jnp.where(lo == 1, hi + 1, 5))
    cls = gidx * N_PAIRS + pair

    onehot = lane == cls
    r_i = lax.broadcasted_iota(I32, (tm, tm), 0)
    c_i = lax.broadcasted_iota(I32, (tm, tm), 1)
    before = jnp.where(c_i < r_i, 1.0, 0.0).astype(BF16)
    excl = _dot(before, jnp.where(onehot, 1.0, 0.0).astype(BF16))
    rank = jnp.sum(jnp.where(onehot, excl + cnt_scr[...], 0.0), axis=-1, keepdims=True)
    cnt_scr[...] = cnt_scr[...] + jnp.sum(jnp.where(onehot, 1.0, 0.0), axis=0, keepdims=True)
    cnt_ref[...] = cnt_scr[...]

    route = jnp.where(lane == 0, cls.astype(F32),
                      jnp.where(lane == 1, w_lo, jnp.where(lane == 2, w_hi, jnp.where(lane == 3, rank, 0.0))))
    xr_ref[:, D_MODEL:] = route


def _mix(x, ya, yr, g_mix, wga, wgr, wba, wbr, wout, g_ffn, wrt, brt):
    n = x.shape[0]
    tm = TOKEN_TILE
    row = lambda i: (i, 0)
    fixed = lambda i: (0, 0)
    full = lambda a: pl.BlockSpec(a.shape, fixed)
    return pl.pallas_call(
        _mix_kernel,
        grid=(n // tm,),
        in_specs=[pl.BlockSpec((tm, D_MODEL), row), pl.BlockSpec((tm, ATT_W), row), pl.BlockSpec((tm, REC_W), row),
                  full(g_mix), full(wga), full(wgr), full(wba), full(wbr), full(wout), full(g_ffn), full(wrt), full(brt)],
        out_specs=[pl.BlockSpec((tm, ROW_W), row), pl.BlockSpec((1, LANES), fixed)],
        out_shape=[jax.ShapeDtypeStruct((n, ROW_W), F32), jax.ShapeDtypeStruct((1, LANES), F32)],
        scratch_shapes=[pltpu.VMEM((1, LANES), F32)],
        compiler_params=_cparams(1),
        name="mix",
    )(x, ya, yr, g_mix, wga, wgr, wba, wbr, wout, g_ffn, wrt, brt)


def _row_copy_out(x_ref, o_hbm, sem, r, p):
    return pltpu.make_async_copy(x_ref.at[pl.ds(r, 1)], o_hbm.at[pl.ds(p, 1)], sem)


def _dispatch_kernel(zs_ref, pos_ref, x_ref, o_hbm, zero_scr, zsem, rsem):
    i = pl.program_id(0)
    tm = x_ref.shape[0]

    def zero_copy(c):
        start = pl.multiple_of(jnp.maximum(zs_ref[c], 0), tm)
        return pltpu.make_async_copy(zero_scr, o_hbm.at[pl.ds(start, tm)], zsem)

    @pl.when(i == 0)
    def _():
        zero_scr[...] = jnp.zeros(zero_scr.shape, F32)
        for c in range(zs_ref.shape[0]):
            @pl.when(zs_ref[c] >= 0)
            def _():
                zero_copy(c).start()
        for c in range(zs_ref.shape[0]):
            @pl.when(zs_ref[c] >= 0)
            def _():
                zero_copy(c).wait()

    def issue(r, carry):
        _row_copy_out(x_ref, o_hbm, rsem, r, pos_ref[0, r]).start()
        return carry

    lax.fori_loop(0, tm, issue, 0, unroll=8)
    pltpu.make_async_copy(x_ref, o_hbm.at[pl.ds(0, tm)], rsem).wait()


def _dispatch(xr, pos2d, zero_starts, n_rows_sorted):
    n = xr.shape[0]
    tm = TOKEN_TILE
    grid_spec = pltpu.PrefetchScalarGridSpec(
        num_scalar_prefetch=1,
        grid=(n // tm,),
        in_specs=[pl.BlockSpec((None, 1, tm), lambda i, zs: (i, 0, 0), memory_space=pltpu.SMEM),
                  pl.BlockSpec((tm, ROW_W), lambda i, zs: (i, 0))],
        out_specs=pl.BlockSpec(memory_space=pl.ANY),
        scratch_shapes=[pltpu.VMEM((tm, ROW_W), F32), pltpu.SemaphoreType.DMA(()), pltpu.SemaphoreType.DMA(())],
    )
    return pl.pallas_call(
        _dispatch_kernel,
        grid_spec=grid_spec,
        out_shape=jax.ShapeDtypeStruct((n_rows_sorted, ROW_W), F32),
        compiler_params=_cparams(1),
        name="dispatch",
    )(zero_starts, pos2d, xr)


def _moe_kernel(blk_ref, elo_ref, ehi_ref, valid_ref, xs_ref, g_ref,
                wg_lo_ref, wg_hi_ref, wu_lo_ref, wu_hi_ref, wd_lo_ref, wd_hi_ref, o_ref):
    j = pl.program_id(0)

    @pl.when(valid_ref[j] == 1)
    def _():
        x = xs_ref[:, :D_MODEL]
        w_lo = xs_ref[:, D_MODEL + 1:D_MODEL + 2]
        w_hi = xs_ref[:, D_MODEL + 2:D_MODEL + 3]
        h = _rms(x, g_ref[...]).astype(BF16)
        hid_lo = (_silu(_dot(h, wg_lo_ref[...])) * _dot(h, wu_lo_ref[...]) * w_lo).astype(BF16)
        hid_hi = (_silu(_dot(h, wg_hi_ref[...])) * _dot(h, wu_hi_ref[...]) * w_hi).astype(BF16)
        o_ref[...] = x + (_dot(hid_lo, wd_lo_ref[...]) + _dot(hid_hi, wd_hi_ref[...]))

    @pl.when(valid_ref[j] == 0)
    def _():
        o_ref[...] = jnp.zeros(o_ref.shape, F32)


def _moe(xs, g_ffn, wg, wu, wd, in_blk, e_lo, e_hi, valid):
    tm = TOKEN_TILE
    n_tiles = xs.shape[0] // tm
    lo = lambda j, blk, elo, ehi, va: (elo[j], 0, 0)
    hi = lambda j, blk, elo, ehi, va: (ehi[j], 0, 0)
    up = pl.BlockSpec((None, D_MODEL, D_EXPERT), lo), pl.BlockSpec((None, D_MODEL, D_EXPERT), hi)
    down = pl.BlockSpec((None, D_EXPERT, D_MODEL), lo), pl.BlockSpec((None, D_EXPERT, D_MODEL), hi)
    grid_spec = pltpu.PrefetchScalarGridSpec(
        num_scalar_prefetch=4,
        grid=(n_tiles,),
        in_specs=[pl.BlockSpec((tm, ROW_W), lambda j, blk, elo, ehi, va: (blk[j], 0)),
                  pl.BlockSpec((1, D_MODEL), lambda j, blk, elo, ehi, va: (0, 0)),
                  up[0], up[1], up[0], up[1], down[0], down[1]],
        out_specs=pl.BlockSpec((tm, D_MODEL), lambda j, blk, elo, ehi, va: (j, 0)),
    )
    return pl.pallas_call(
        _moe_kernel,
        grid_spec=grid_spec,
        out_shape=jax.ShapeDtypeStruct((xs.shape[0], D_MODEL), F32),
        compiler_params=_cparams(1),
        name="moe",
    )(in_blk, e_lo, e_hi, valid, xs, g_ffn, wg, wg, wu, wu, wd, wd)


def _ple_kernel(pos_ref, pos_next_ref, xs_hbm, p_ref, gple_ref, wpg_ref, wp_ref, gfin_ref, y_ref, buf, sem):
    i = pl.program_id(0)
    tm = buf.shape[1]
    slot = i % 2

    def gather(idx_ref, s):
        def issue(r, carry):
            pltpu.make_async_copy(xs_hbm.at[pl.ds(idx_ref[0, r], 1)], buf.at[s, pl.ds(r, 1)], sem.at[s]).start()
            return carry

        lax.fori_loop(0, tm, issue, 0, unroll=8)

    @pl.when(i == 0)
    def _():
        gather(pos_ref, 0)

    @pl.when(i + 1 < pl.num_programs(0))
    def _():
        gather(pos_next_ref, 1 - slot)

    pltpu.make_async_copy(xs_hbm.at[pl.ds(0, tm)], buf.at[slot], sem.at[slot]).wait()

    x2 = buf[slot]
    hp = _rms(x2, gple_ref[...]).astype(BF16)
    gate = jax.nn.sigmoid(_dot(hp, wpg_ref[...]))
    x3 = x2 + gate * _dot(p_ref[...].astype(BF16), wp_ref[...])
    y_ref[...] = _rms(x3, gfin_ref[...])


def _ple(xs2, pos2d, p, g_ple, wpg, wp, g_final):
    n = p.shape[0]
    tm = TOKEN_TILE
    row = lambda i: (i, 0)
    fixed = lambda i: (0, 0)
    last = n // tm - 1
    return pl.pallas_call(
        _ple_kernel,
        grid=(n // tm,),
        in_specs=[pl.BlockSpec((None, 1, tm), lambda i: (i, 0, 0), memory_space=pltpu.SMEM),
                  pl.BlockSpec((None, 1, tm), lambda i: (jnp.minimum(i + 1, last), 0, 0), memory_space=pltpu.SMEM),
                  pl.BlockSpec(memory_space=pl.ANY),
                  pl.BlockSpec((tm, P_DIM), row), pl.BlockSpec((1, D_MODEL), fixed),
                  pl.BlockSpec((D_MODEL, D_MODEL), fixed), pl.BlockSpec((P_DIM, D_MODEL), fixed),
                  pl.BlockSpec((1, D_MODEL), fixed)],
        out_specs=pl.BlockSpec((tm, D_MODEL), row),
        out_shape=jax.ShapeDtypeStruct((n, D_MODEL), F32),
        scratch_shapes=[pltpu.VMEM((2, tm, D_MODEL), F32), pltpu.SemaphoreType.DMA((2,))],
        compiler_params=_cparams(1),
        name="ple",
    )(pos2d, pos2d, xs2, p, g_ple, wpg, wp, g_final)


def _routing_plan(xr, counts, n):
    tm = TOKEN_TILE
    n_tiles = n // tm + N_CLASSES
    cls = xr[:, D_MODEL].astype(I32)
    rank = xr[:, D_MODEL + 3].astype(I32)
    cnt = counts[0, :N_CLASSES].astype(I32)
    tiles = (cnt + tm - 1) // tm
    tile_end = jnp.cumsum(tiles)
    tile_start = tile_end - tiles
    n_used = tile_end[-1]
    pos = (tile_start * tm)[cls] + rank
    j = jnp.arange(n_tiles, dtype=I32)
    valid = j < n_used
    in_blk = jnp.minimum(j, n_used - 1)
    tcls = jnp.sum((tile_end[None, :] <= in_blk[:, None]).astype(I32), axis=1)
    grp = tcls // N_PAIRS
    pair = tcls % N_PAIRS
    e_lo = grp * EXPERTS_PER_GROUP + jnp.asarray(PAIR_LO, I32)[pair]
    e_hi = grp * EXPERTS_PER_GROUP + jnp.asarray(PAIR_HI, I32)[pair]
    seg_zero = jnp.where(tiles > 0, tile_end * tm - tm, -1)
    tail = n_used + jnp.arange(N_CLASSES, dtype=I32)
    tail_zero = jnp.where(tail < n_tiles, tail * tm, -1)
    zero_starts = jnp.concatenate([seg_zero, tail_zero]).astype(I32)
    return pos.reshape(n // tm, 1, tm), zero_starts, in_blk, e_lo, e_hi, valid.astype(I32), n_tiles * tm


def _rope_tables(pos, rows):
    half = DH_A // 2
    inv_freq = ROPE_THETA ** (-jnp.arange(half, dtype=F32) / half)
    ang = pos.astype(F32)[:, None] * inv_freq[None, :]
    cos = jnp.cos(ang)
    sin = jnp.sin(ang)
    cos_t = jnp.tile(jnp.concatenate([cos, cos], axis=-1), (1, ATT_W // DH_A))
    sin_t = jnp.tile(jnp.concatenate([-sin, sin], axis=-1), (1, ATT_W // DH_A))
    reps = max(rows // pos.shape[0], 1)
    return jnp.tile(cos_t, (reps, 1)), jnp.tile(sin_t, (reps, 1))


def _layer(i, x, p, pos, attend, s0, w, batch, seq, transposed):
    n = batch * seq
    small = BF16 if seq % 16 == 0 else F32
    cos_t, sin_t = _rope_tables(pos, TOKEN_TILE)
    proj = _inproj(x, w["g_mix"], w["w_seq"], cos_t, sin_t, batch, seq, transposed)
    lam_init = 0.8 - 0.6 * math.exp(-0.3 * i)
    ya, kf, vf = attend(proj, lam_init)
    qr, fl, ir, gr = proj[-4:]
    yr, s_new = _hgrn(qr, fl, ir, gr, w["lb"], w["g_rec"], s0, batch, seq, small)
    xr, counts = _mix(x, ya.astype(BF16), yr.astype(BF16), w["g_mix"], w["w_gate_a"], w["w_gate_r"], w["w_branch_a"],
                      w["w_branch_r"], w["w_out"], w["g_ffn"], w["w_route"], w["b_route"])
    pos2d, zero_starts, in_blk, e_lo, e_hi, valid, n_sorted = _routing_plan(xr, counts, n)
    xs = _dispatch(xr, pos2d, zero_starts, n_sorted)
    xs2 = _moe(xs, w["g_ffn"], w["w_exp_gate"], w["w_exp_up"], w["w_exp_down"], in_blk, e_lo, e_hi, valid)
    y = _ple(xs2, pos2d, p, w["g_ple"], w["w_ple_gate"], w["w_ple"], w["g_final"])
    return y, kf, vf, s_new


def kernel(x_prompt, x_sample, p_prompt, p_sample, cache_k, cache_v, state_hgrn, page_table, g_mix, w_in, lam,
           g_subln, lb_param, g_rec, w_branch_a, w_branch_r, w_out, g_ffn, w_route_group, b_route_group,
           w_route_expert, b_route_expert, w_exp_gate, w_exp_up, w_exp_down, g_ple, w_ple_gate, w_ple, g_final):
    depth = w_in.shape[0]
    assert depth == 1, "single-layer step"
    bp, sp, _ = x_prompt.shape
    bs, ts, _ = x_sample.shape
    past_len = page_table.shape[1] * PAGE_SIZE
    i = 0

    w_in_b = w_in[i].astype(BF16)
    n_route = N_GROUPS + N_EXPERTS
    w_route = jnp.concatenate([w_route_group[i], w_route_expert[i]], axis=1)
    w_route = jnp.pad(w_route, ((0, 0), (0, LANES - n_route))).astype(BF16)
    b_route = jnp.pad(jnp.concatenate([b_route_group[i], b_route_expert[i]]), (0, LANES - n_route)).reshape(1, LANES)
    lb = jnp.cumsum(jax.nn.softmax(lb_param.astype(F32), axis=0), axis=0)[i].reshape(1, REC_W)
    w = dict(
        g_mix=g_mix[i].reshape(1, D_MODEL), w_seq=w_in_b[:, :N_SEQ_COLS],
        w_gate_a=w_in_b[:, N_SEQ_COLS:N_SEQ_COLS + D_MODEL], w_gate_r=w_in_b[:, N_SEQ_COLS + D_MODEL:],
        lb=lb, g_rec=g_rec[i].reshape(1, V_R),
        w_branch_a=w_branch_a[i].astype(BF16), w_branch_r=w_branch_r[i].astype(BF16), w_out=w_out[i].astype(BF16),
        g_ffn=g_ffn[i].reshape(1, D_MODEL), w_route=w_route, b_route=b_route,
        w_exp_gate=w_exp_gate[i].astype(BF16), w_exp_up=w_exp_up[i].astype(BF16), w_exp_down=w_exp_down[i].astype(BF16),
        g_ple=g_ple[i].reshape(1, D_MODEL), w_ple_gate=w_ple_gate[i].astype(BF16), w_ple=w_ple[i].astype(BF16),
        g_final=g_final.reshape(1, D_MODEL),
    )
    lam_i = lam[i].astype(F32)
    g_sub = g_subln[i].reshape(1, E_A)

    def attend_prompt(proj, lam_init):
        q_t, k_t, kb, vf, v_t = proj[:5]
        return _attn_prompt(q_t, kb, v_t, lam_i, g_sub.reshape(E_A, 1), bp, sp, lam_init), k_t, vf

    n_phys = cache_k.shape[1]
    ckt = jnp.transpose(cache_k[i], (0, 2, 3, 4, 1)).reshape(n_phys, ATT_W, PAGE_SIZE)
    cv2 = cache_v[i].reshape(n_phys, PAGE_SIZE * H_A, E_A)

    def attend_sample(proj, lam_init):
        q, kf, vf = proj[:3]
        return _attn_sample(q, kf, vf, ckt, cv2, page_table, lam_i, g_sub, bs, ts, lam_init), kf, vf

    pos_p = jnp.arange(sp, dtype=I32)
    pos_s = past_len + jnp.arange(ts, dtype=I32)
    s0_p = jnp.zeros((bp, H_R, K_R, V_R), F32)

    y_p, k_t, v_p, s_p = _layer(i, x_prompt.reshape(bp * sp, D_MODEL), p_prompt[i].reshape(bp * sp, P_DIM),
                                pos_p, attend_prompt, s0_p, w, bp, sp, True)
    y_s, k_s, v_s, s_s = _layer(i, x_sample.reshape(bs * ts, D_MODEL), p_sample[i].reshape(bs * ts, P_DIM),
                                pos_s, attend_sample, state_hgrn[i], w, bs, ts, False)
    k_p = jnp.transpose(k_t.reshape(1, bp, H_A, 2, DH_A, sp), (0, 1, 5, 2, 3, 4))

    return (y_p.reshape(bp, sp, D_MODEL), y_s.reshape(bs, ts, D_MODEL),
            k_p, v_p.reshape(1, bp, sp, H_A, E_A), s_p.reshape(1, bp, H_R, K_R, V_R),
            k_s.reshape(1, bs, ts, H_A, 2, DH_A), v_s.reshape(1, bs, ts, H_A, E_A), s_s.reshape(1, bs, H_R, K_R, V_R))
```

```python
import functools
import math

import jax
import jax.numpy as jnp
import numpy as np
from jax import lax
from jax.experimental import pallas as pl
from jax.experimental.pallas import tpu as pltpu

F32 = jnp.float32
BF16 = jnp.bfloat16
I32 = jnp.int32

D_MODEL = 1024
P_DIM = 256
H_A = 4
DH_A = 64
E_A = 2 * DH_A
H_R = 4
K_R = 128
V_R = 128
HGRN_CHUNK = 32
N_GROUPS = 4
EXPERTS_PER_GROUP = 4
N_EXPERTS = N_GROUPS * EXPERTS_PER_GROUP
D_EXPERT = 256
ROPE_THETA = 10000.0
EPS = 1e-6
NEG_INF = -1e30
LOG2_E = 1.4426950408889634
PAGE_SIZE = 128
ATT_W = H_A * E_A
REC_W = H_R * V_R
N_SEQ_COLS = 7 * 512
LANES = 128
N_PAIRS = 6
N_CLASSES = N_GROUPS * N_PAIRS
PAIR_LO = (0, 0, 0, 1, 1, 2)
PAIR_HI = (1, 2, 3, 2, 3, 3)
ROW_W = D_MODEL + LANES
TOKEN_TILE = 256
MIX_TILE = 512
ATTN_TILE = 512
PAGES_PER_STEP = 16
VMEM_LIMIT = 56 * 1024 * 1024


def _cparams(n_axes, flags=None):
    return pltpu.CompilerParams(dimension_semantics=("arbitrary",) * n_axes, vmem_limit_bytes=VMEM_LIMIT, flags=flags)


def _rms(x, g):
    return x * lax.rsqrt(jnp.mean(x * x, axis=-1, keepdims=True) + EPS) * g


def _silu(x):
    return x * jax.nn.sigmoid(x)


def _dot(a, b):
    return jnp.dot(a, b, preferred_element_type=F32)


def _dot_nt(a, b):
    return lax.dot_general(a, b, (((1,), (1,)), ((), ())), preferred_element_type=F32)


def _inproj_kernel(x_ref, g_ref, w_ref, cos_ref, sin_ref, *out_refs, transposed):
    h = _rms(x_ref[...], g_ref[...]).astype(BF16)
    cos = cos_ref[...]
    sin = sin_ref[...]
    tm = h.shape[0]
    lane = lax.broadcasted_iota(I32, (tm, ATT_W), 1)
    first_half = (lane % DH_A) < (DH_A // 2)

    def proj(c):
        return _dot(h, w_ref[:, c * 512:(c + 1) * 512])

    def rope(t):
        rot = jnp.where(first_half, pltpu.roll(t, ATT_W - DH_A // 2, 1), pltpu.roll(t, DH_A // 2, 1))
        return t * cos + rot * sin

    q = rope(proj(0)) * (DH_A ** -0.5)
    k = rope(proj(1))
    v = proj(2)
    if transposed:
        qt_ref, kt_ref, kb_ref, vf_ref, vt_ref = out_refs[:5]
        qt_ref[...] = (q * LOG2_E).T.astype(BF16)
        kt_ref[...] = k.T
        kb_ref[...] = k.astype(BF16)
        vf_ref[...] = v
        vt_ref[...] = v.T.astype(BF16)
        rest = out_refs[5:]
    else:
        q_ref, kf_ref, vf_ref = out_refs[:3]
        q_ref[...] = q
        kf_ref[...] = k
        vf_ref[...] = v
        rest = out_refs[3:]
    qr_ref, fl_ref, ir_ref, gr_ref = rest
    qr_ref[...] = _silu(proj(3))
    fl_ref[...] = proj(4)
    ir_ref[...] = proj(5)
    gr_ref[...] = proj(6)


def _inproj(x, g_mix, w_seq, cos_t, sin_t, batch, seq, transposed):
    n = x.shape[0]
    tm = TOKEN_TILE
    nblk = max(seq // tm, 1)
    row = lambda i: (i, 0)
    tab = lambda i: (i % nblk, 0)
    const = lambda i: (0, 0)
    o512 = pl.BlockSpec((tm, 512), row)
    sds = lambda dt: jax.ShapeDtypeStruct((n, 512), dt)
    if transposed:
        ot = pl.BlockSpec((None, 512, tm), lambda i: (i // nblk, 0, i % nblk))
        sdt = lambda dt: jax.ShapeDtypeStruct((batch, 512, seq), dt)
        out_specs = [ot, ot, o512, o512, ot] + [o512] * 4
        out_shape = [sdt(BF16), sdt(F32), sds(BF16), sds(F32), sdt(BF16)] + [sds(F32)] * 4
    else:
        out_specs = [o512] * 7
        out_shape = [sds(F32)] * 7
    return pl.pallas_call(
        functools.partial(_inproj_kernel, transposed=transposed),
        grid=(n // tm,),
        in_specs=[pl.BlockSpec((tm, D_MODEL), row), pl.BlockSpec((1, D_MODEL), const),
                  pl.BlockSpec((D_MODEL, N_SEQ_COLS), const),
                  pl.BlockSpec((tm, ATT_W), tab), pl.BlockSpec((tm, ATT_W), tab)],
        out_specs=out_specs,
        out_shape=out_shape,
        compiler_params=_cparams(1),
        name="inproj",
    )(x, g_mix, w_seq, cos_t, sin_t)


def _lambda_value(lam_ref):
    lam = lam_ref[...]
    s01 = jnp.sum(lam[0:1, :] * lam[1:2, :], axis=-1, keepdims=True)
    s23 = jnp.sum(lam[2:3, :] * lam[3:4, :], axis=-1, keepdims=True)
    return jnp.exp(s01) - jnp.exp(s23)


def _diff_norm(o0, o1, lam_val, g_sub, lam_init):
    od = o0 - lam_val * o1
    return _rms(od, g_sub) * (1.0 - lam_init)


def _split_maps(qh):
    lane = lax.broadcasted_iota(I32, qh.shape, 1)
    zero = jnp.zeros_like(qh)
    return jnp.concatenate([jnp.where(lane < DH_A, qh, zero), jnp.where(lane >= DH_A, qh, zero)], axis=0)


def _attn_prompt_kernel(q_ref, k_ref, v_ref, lam_ref, g_ref, o_ref, q2_scr, m_scr, l_scr, acc_scr, *, lam_init):
    i = pl.program_id(1)
    tq = q_ref.shape[1]
    feat = lax.broadcasted_iota(I32, (E_A, tq), 0)
    for h in range(H_A):
        qh = q_ref[h * E_A:(h + 1) * E_A, :].astype(F32)
        q2_scr[h] = jnp.concatenate([jnp.where(feat < DH_A, qh, 0.0), jnp.where(feat >= DH_A, qh, 0.0)],
                                    axis=1).astype(BF16)
    m_scr[...] = jnp.full(m_scr.shape, NEG_INF, F32)
    l_scr[...] = jnp.zeros(l_scr.shape, F32)
    acc_scr[...] = jnp.zeros(acc_scr.shape, F32)

    def key_block(kb, on_diagonal):
        r0 = pl.multiple_of(kb * tq, tq)
        for h in range(H_A):
            cs = slice(h * E_A, (h + 1) * E_A)
            s = _dot(k_ref[pl.ds(r0, tq), cs], q2_scr[h])
            if on_diagonal:
                key = lax.broadcasted_iota(I32, (tq, 2 * tq), 0)
                qry = lax.broadcasted_iota(I32, (tq, 2 * tq), 1) % tq
                s = jnp.where(key > qry, NEG_INF, s)
            m_old = m_scr[h]
            m_new = jnp.maximum(m_old, jnp.max(s, axis=0, keepdims=True))
            alpha = jnp.exp2(m_old - m_new)
            p = jnp.exp2(s - m_new)
            l_scr[h] = l_scr[h] * alpha + jnp.sum(p, axis=0, keepdims=True)
            acc_scr[h] = acc_scr[h] * alpha + _dot(v_ref[cs, pl.ds(r0, tq)], p.astype(BF16))
            m_scr[h] = m_new

    def below_diagonal(kb, carry):
        key_block(kb, False)
        return carry

    lax.fori_loop(0, i, below_diagonal, 0)
    key_block(i, True)

    lam_val = _lambda_value(lam_ref) + lam_init
    for h in range(H_A):
        o = acc_scr[h] / l_scr[h]
        od = o[:, :tq] - lam_val * o[:, tq:]
        ms = jnp.mean(od * od, axis=0, keepdims=True)
        y_t = od * lax.rsqrt(ms + EPS) * g_ref[...] * (1.0 - lam_init)
        o_ref[:, h * E_A:(h + 1) * E_A] = y_t.T.astype(o_ref.dtype)


def _attn_prompt(q_t, k, v_t, lam, g_sub_col, batch, seq, lam_init):
    tq = min(ATTN_TILE, seq)
    nq = seq // tq
    return pl.pallas_call(
        functools.partial(_attn_prompt_kernel, lam_init=lam_init),
        grid=(batch, nq),
        in_specs=[pl.BlockSpec((None, ATT_W, tq), lambda b, i: (b, 0, i)),
                  pl.BlockSpec((seq, ATT_W), lambda b, i: (b, 0)),
                  pl.BlockSpec((None, ATT_W, seq), lambda b, i: (b, 0, 0)),
                  pl.BlockSpec((4, DH_A), lambda b, i: (0, 0)),
                  pl.BlockSpec((E_A, 1), lambda b, i: (0, 0))],
        out_specs=pl.BlockSpec((tq, ATT_W), lambda b, i: (b * nq + i, 0)),
        out_shape=jax.ShapeDtypeStruct(k.shape, BF16),
        scratch_shapes=[pltpu.VMEM((H_A, E_A, 2 * tq), BF16), pltpu.VMEM((H_A, 1, 2 * tq), F32),
                        pltpu.VMEM((H_A, 1, 2 * tq), F32), pltpu.VMEM((H_A, E_A, 2 * tq), F32)],
        compiler_params=_cparams(2),
        name="attn_prompt",
    )(q_t, k, v_t, lam, g_sub_col)


def _attn_sample_kernel(pt_ref, q_ref, kn_ref, vn_ref, lam_ref, g_ref, *refs, lam_init, pages):
    k_refs = refs[:pages]
    v_refs = refs[pages:2 * pages]
    o_ref = refs[2 * pages]
    m_scr, l_scr, acc_scr = refs[2 * pages + 1:]
    j = pl.program_id(1)
    t = q_ref.shape[0]

    @pl.when(j == 0)
    def _():
        m_scr[...] = jnp.full(m_scr.shape, NEG_INF, F32)
        l_scr[...] = jnp.zeros(l_scr.shape, F32)
        acc_scr[...] = jnp.zeros(acc_scr.shape, F32)

    rows_h = 2 * t

    def own_head_blocks(big):
        return jnp.concatenate([big[h * rows_h:(h + 1) * rows_h, h * E_A:(h + 1) * E_A] for h in range(H_A)], axis=0)

    def update(s, v_all):
        m_old = m_scr[...]
        m_new = jnp.maximum(m_old, jnp.max(s, axis=-1, keepdims=True))
        alpha = jnp.exp(m_old - m_new)
        p = jnp.exp(s - m_new)
        l_scr[...] = l_scr[...] * alpha + jnp.sum(p, axis=-1, keepdims=True)
        acc_scr[...] = acc_scr[...] * alpha + own_head_blocks(_dot(p.astype(BF16), v_all))
        m_scr[...] = m_new

    q = q_ref[...]
    zero = jnp.zeros((rows_h, E_A), F32)
    qall = jnp.concatenate(
        [jnp.concatenate([_split_maps(q[:, h * E_A:(h + 1) * E_A]) if hh == h else zero for hh in range(H_A)], axis=1)
         for h in range(H_A)], axis=0).astype(BF16)

    kt = jnp.concatenate([k_refs[r][...].astype(BF16) for r in range(pages)], axis=1)
    v_all = jnp.concatenate(
        [jnp.concatenate([v_refs[r][pl.ds(h, PAGE_SIZE, stride=H_A), :] for h in range(H_A)], axis=1)
         for r in range(pages)], axis=0).astype(BF16)
    update(_dot(qall, kt), v_all)

    @pl.when(j == pl.num_programs(1) - 1)
    def _():
        lam_val = _lambda_value(lam_ref) + lam_init
        n_rows = H_A * rows_h
        row_t = lax.broadcasted_iota(I32, (n_rows, PAGE_SIZE), 0) % t
        col = lax.broadcasted_iota(I32, (n_rows, PAGE_SIZE), 1)
        visible = col <= row_t
        pad = jnp.zeros((PAGE_SIZE - t, ATT_W), F32)
        kn_t = jnp.concatenate([kn_ref[...], pad], axis=0).T.astype(BF16)
        vn = jnp.concatenate([vn_ref[...], pad], axis=0).astype(BF16)
        update(jnp.where(visible, _dot(qall, kn_t), NEG_INF), vn)
        o = acc_scr[...] / l_scr[...]
        for h in range(H_A):
            o0 = o[h * rows_h:h * rows_h + t]
            o1 = o[h * rows_h + t:(h + 1) * rows_h]
            y = _diff_norm(o0, o1, lam_val, g_ref[...], lam_init)
            o_ref[:, h * E_A:(h + 1) * E_A] = y.astype(BF16).astype(o_ref.dtype)


def _attn_sample(q, kn, vn, cache_kt, cache_v2, page_table, lam, g_sub, batch, t, lam_init):
    pages = PAGES_PER_STEP
    n_pages = page_table.shape[1]
    steps = n_pages // pages
    fixed = lambda b, j, pt: (0, 0)
    rowb = lambda b, j, pt: (b, 0)
    n_rows = H_A * 2 * t

    def page_spec(r):
        return pl.BlockSpec((None, ATT_W, PAGE_SIZE), lambda b, j, pt: (pt[b, j * pages + r], 0, 0))

    grid_spec = pltpu.PrefetchScalarGridSpec(
        num_scalar_prefetch=1,
        grid=(batch, steps),
        in_specs=[pl.BlockSpec((t, ATT_W), rowb), pl.BlockSpec((t, ATT_W), rowb), pl.BlockSpec((t, ATT_W), rowb),
                  pl.BlockSpec((4, DH_A), fixed), pl.BlockSpec((1, E_A), fixed)]
                 + [page_spec(r) for r in range(pages)] + [page_spec(r) for r in range(pages)],
        out_specs=pl.BlockSpec((t, ATT_W), rowb),
        scratch_shapes=[pltpu.VMEM((n_rows, 1), F32), pltpu.VMEM((n_rows, 1), F32), pltpu.VMEM((n_rows, E_A), F32)],
    )
    return pl.pallas_call(
        functools.partial(_attn_sample_kernel, lam_init=lam_init, pages=pages),
        grid_spec=grid_spec,
        out_shape=jax.ShapeDtypeStruct(q.shape, F32),
        compiler_params=_cparams(2),
        name="attn_sample",
    )(page_table, q, kn, vn, lam, g_sub, *([cache_kt] * pages), *([cache_v2] * pages))


def _split3(x):
    a = x.astype(BF16)
    r = x - a.astype(F32)
    b = r.astype(BF16)
    c = (r - b.astype(F32)).astype(BF16)
    return a, b, c


def _hgrn_kernel(qr_ref, fl_ref, ir_ref, gr_ref, lb_ref, g_ref, s0_ref, y_ref, s_ref,
                 qe_scr, kk_scr, dl_scr, o_scr, st_scr, *, chunk, blk):
    t = qr_ref.shape[0]
    lb = lb_ref[...]
    row = lax.broadcasted_iota(I32, (blk, blk), 0)
    col = lax.broadcasted_iota(I32, (blk, blk), 1)
    same = (row // chunk) == (col // chunk)
    causal = jnp.logical_and(same, col <= row)
    cum_mask = jnp.where(causal, 1.0, 0.0).astype(BF16)
    all_mask = jnp.where(same, 1.0, 0.0).astype(BF16)

    def gates(b, carry):
        rs = pl.ds(pl.multiple_of(b * blk, blk), blk)
        f = lb + (1.0 - lb) * jax.nn.sigmoid(fl_ref[rs, :])
        g = jnp.log(f)
        k = 1.0 - f
        g1, g2, g3 = _split3(g)
        bcum = _dot(cum_mask, g1) + _dot(cum_mask, g2) + _dot(cum_mask, g3)
        blast = _dot(all_mask, g1) + _dot(all_mask, g2) + _dot(all_mask, g3)
        qe = (qr_ref[rs, :] * jnp.exp(bcum)).astype(BF16)
        kd = (k * jnp.exp(-bcum)).astype(BF16)
        qe_scr[rs, :] = qe.astype(qe_scr.dtype)
        kk_scr[rs, :] = (k * jnp.exp(blast - bcum)).astype(kk_scr.dtype)
        dl_scr[rs, :] = jnp.exp(blast)
        v = ir_ref[rs, :].astype(BF16)
        for h in range(H_R):
            cs = slice(h * K_R, (h + 1) * K_R)
            a = jnp.where(causal, _dot_nt(qe[:, cs], kd[:, cs]), 0.0).astype(BF16)
            o_scr[rs, cs] = _dot(a, v[:, cs])
        return carry

    lax.fori_loop(0, t // blk, gates, 0)

    for h in range(H_R):
        st_scr[h] = s0_ref[h].T

    def step(j, carry):
        r0 = pl.multiple_of(j * chunk, chunk)
        rows = pl.ds(r0, chunk)
        for h in range(H_R):
            cs = slice(h * K_R, (h + 1) * K_R)
            st = st_scr[h]
            o_scr[rows, cs] = o_scr[rows, cs] + _dot_nt(qe_scr[rows, cs].astype(BF16), st.astype(BF16))
            ut = _dot(ir_ref[rows, cs].T.astype(BF16), kk_scr[rows, cs].astype(BF16))
            st_scr[h] = st * dl_scr[pl.ds(r0, 1), cs] + ut
        return carry

    lax.fori_loop(0, t // chunk, step, 0, unroll=2 if t // chunk >= 2 else 1)
    for h in range(H_R):
        s_ref[h] = st_scr[h].T

    g_rec = g_ref[...]

    def finish(b, carry):
        rs = pl.ds(pl.multiple_of(b * blk, blk), blk)
        for h in range(H_R):
            cs = slice(h * K_R, (h + 1) * K_R)
            y = _rms(o_scr[rs, cs], g_rec) * _silu(gr_ref[rs, cs])
            y_ref[rs, cs] = y.astype(BF16).astype(y_ref.dtype)
        return carry

    lax.fori_loop(0, t // blk, finish, 0)


def _hgrn(qr, fl, ir, gr, lb, g_rec, s0, batch, t, out_dtype):
    chunk = math.gcd(t, HGRN_CHUNK)
    blk = min(t, 256)
    rowb = lambda b: (b, 0)
    fixed = lambda b: (0, 0)
    seq = pl.BlockSpec((t, REC_W), rowb)
    state = pl.BlockSpec((None, H_R, K_R, V_R), lambda b: (b, 0, 0, 0))
    return pl.pallas_call(
        functools.partial(_hgrn_kernel, chunk=chunk, blk=blk),
        grid=(batch,),
        in_specs=[seq, seq, seq, seq, pl.BlockSpec((1, REC_W), fixed), pl.BlockSpec((1, V_R), fixed), state],
        out_specs=[seq, state],
        out_shape=[jax.ShapeDtypeStruct(qr.shape, out_dtype), jax.ShapeDtypeStruct(s0.shape, F32)],
        scratch_shapes=[pltpu.VMEM((t, REC_W), out_dtype), pltpu.VMEM((t, REC_W), out_dtype),
                        pltpu.VMEM((t, REC_W), F32), pltpu.VMEM((t, REC_W), F32), pltpu.VMEM((H_R, V_R, K_R), F32)],
        compiler_params=_cparams(1),
        name="hgrn",
    )(qr, fl, ir, gr, lb, g_rec, s0)


def _mix_kernel(x_ref, ya_ref, yr_ref, gmix_ref, wga_ref, wgr_ref, wba_ref, wbr_ref, wout_ref,
                gffn_ref, wrt_ref, brt_ref, xr_ref, cnt_ref, cnt_scr):
    i = pl.program_id(0)
    tm = x_ref.shape[0]

    @pl.when(i == 0)
    def _():
        cnt_scr[...] = jnp.zeros(cnt_scr.shape, F32)

    x = x_ref[...]
    h = _rms(x, gmix_ref[...]).astype(BF16)
    merged = (jax.nn.sigmoid(_dot(h, wga_ref[...])) * _dot(ya_ref[...], wba_ref[...])
              + jax.nn.sigmoid(_dot(h, wgr_ref[...])) * _dot(yr_ref[...], wbr_ref[...]))
    x1 = x + _dot(merged.astype(BF16), wout_ref[...])
    xr_ref[:, :D_MODEL] = x1

    hf = _rms(x1, gffn_ref[...]).astype(BF16)
    lg = _dot(hf, wrt_ref[...]) + brt_ref[...]
    lane = lax.broadcasted_iota(I32, (tm, LANES), 1)
    is_group = lane < N_GROUPS
    mg = jnp.max(jnp.where(is_group, lg, NEG_INF), axis=-1, keepdims=True)
    gidx = jnp.min(jnp.where(jnp.logical_and(is_group, lg == mg), lane, LANES), axis=-1, keepdims=True)
    p_top = 1.0 / jnp.sum(jnp.where(is_group, jnp.exp(lg - mg), 0.0), axis=-1, keepdims=True)
    base = N_GROUPS + EXPERTS_PER_GROUP * gidx
    in_grp = jnp.logical_and(lane >= base, lane < base + EXPERTS_PER_GROUP)
    v1 = jnp.max(jnp.where(in_grp, lg, NEG_INF), axis=-1, keepdims=True)
    e1 = jnp.min(jnp.where(jnp.logical_and(in_grp, lg == v1), lane, LANES), axis=-1, keepdims=True)
    rest = jnp.logical_and(in_grp, lane != e1)
    v2 = jnp.max(jnp.where(rest, lg, NEG_INF), axis=-1, keepdims=True)
    e2 = jnp.min(jnp.where(jnp.logical_and(rest, lg == v2), lane, LANES), axis=-1, keepdims=True)
    tt = jnp.exp(v2 - v1)
    w_a = (1.0 / (1.0 + tt)) * p_top
    w_b = (tt / (1.0 + tt)) * p_top
    a = e1 - base
    b = e2 - base
    a_first = a < b
    lo = jnp.minimum(a, b)
    hi = jnp.maximum(a, b)
    w_lo = jnp.where(a_first, w_a, w_b)
    w_hi = jnp.where(a_first, w_b, w_a)
    pair = jnp.where(lo == 0, hi - 1, jnp.where(lo == 1, hi + 1, 5))
    cls = gidx * N_PAIRS + pair

    onehot = lane == cls
    r_i = lax.broadcasted_iota(I32, (tm, tm), 0)
    c_i = lax.broadcasted_iota(I32, (tm, tm), 1)
    before = jnp.where(c_i < r_i, 1.0, 0.0).astype(BF16)
    excl = _dot(before, jnp.where(onehot, 1.0, 0.0).astype(BF16))
    rank = jnp.sum(jnp.where(onehot, excl + cnt_scr[...], 0.0), axis=-1, keepdims=True)
    cnt_scr[...] = cnt_scr[...] + jnp.sum(jnp.where(onehot, 1.0, 0.0), axis=0, keepdims=True)
    cnt_ref[...] = cnt_scr[...]

    route = jnp.where(lane == 0, cls.astype(F32),
                      jnp.where(lane == 1, w_lo, jnp.where(lane == 2, w_hi, jnp.where(lane == 3, rank, 0.0))))
    xr_ref[:, D_MODEL:] = route


def _mix(x, ya, yr, g_mix, wga, wgr, wba, wbr, wout, g_ffn, wrt, brt):
    n = x.shape[0]
    tm = min(MIX_TILE, n)
    row = lambda i: (i, 0)
    fixed = lambda i: (0, 0)
    full = lambda a: pl.BlockSpec(a.shape, fixed)
    return pl.pallas_call(
        _mix_kernel,
        grid=(n // tm,),
        in_specs=[pl.BlockSpec((tm, D_MODEL), row), pl.BlockSpec((tm, ATT_W), row), pl.BlockSpec((tm, REC_W), row),
                  full(g_mix), full(wga), full(wgr), full(wba), full(wbr), full(wout), full(g_ffn), full(wrt), full(brt)],
        out_specs=[pl.BlockSpec((tm, ROW_W), row), pl.BlockSpec((1, LANES), fixed)],
        out_shape=[jax.ShapeDtypeStruct((n, ROW_W), F32), jax.ShapeDtypeStruct((1, LANES), F32)],
        scratch_shapes=[pltpu.VMEM((1, LANES), F32)],
        compiler_params=_cparams(1),
        name="mix",
    )(x, ya, yr, g_mix, wga, wgr, wba, wbr, wout, g_ffn, wrt, brt)


def _row_copy_out(x_ref, o_hbm, sem, r, p):
    return pltpu.make_async_copy(x_ref.at[pl.ds(r, 1)], o_hbm.at[pl.ds(p, 1)], sem)


def _dispatch_kernel(zs_ref, pos_ref, x_ref, o_hbm, zero_scr, zsem, rsem):
    i = pl.program_id(0)
    tm = x_ref.shape[0]

    def zero_copy(c):
        start = pl.multiple_of(jnp.maximum(zs_ref[c], 0), tm)
        return pltpu.make_async_copy(zero_scr, o_hbm.at[pl.ds(start, tm)], zsem)

    @pl.when(i == 0)
    def _():
        zero_scr[...] = jnp.zeros(zero_scr.shape, F32)
        for c in range(zs_ref.shape[0]):
            @pl.when(zs_ref[c] >= 0)
            def _():
                zero_copy(c).start()
        for c in range(zs_ref.shape[0]):
            @pl.when(zs_ref[c] >= 0)
            def _():
                zero_copy(c).wait()

    for r in range(tm):
        _row_copy_out(x_ref, o_hbm, rsem, r, pos_ref[0, r]).start(priority=r % 2)
    pltpu.make_async_copy(x_ref, o_hbm.at[pl.ds(0, tm)], rsem).wait()


def _dispatch(xr, pos2d, zero_starts, n_rows_sorted):
    n = xr.shape[0]
    tm = TOKEN_TILE
    grid_spec = pltpu.PrefetchScalarGridSpec(
        num_scalar_prefetch=1,
        grid=(n // tm,),
        in_specs=[pl.BlockSpec((None, 1, tm), lambda i, zs: (i, 0, 0), memory_space=pltpu.SMEM),
                  pl.BlockSpec((tm, ROW_W), lambda i, zs: (i, 0))],
        out_specs=pl.BlockSpec(memory_space=pl.ANY),
        scratch_shapes=[pltpu.VMEM((tm, ROW_W), F32), pltpu.SemaphoreType.DMA(()), pltpu.SemaphoreType.DMA(())],
    )
    return pl.pallas_call(
        _dispatch_kernel,
        grid_spec=grid_spec,
        out_shape=jax.ShapeDtypeStruct((n_rows_sorted, ROW_W), F32),
        compiler_params=_cparams(1),
        name="dispatch",
    )(zero_starts, pos2d, xr)


def _moe_kernel(blk_ref, elo_ref, ehi_ref, valid_ref, xs_ref, g_ref,
                wg_lo_ref, wg_hi_ref, wu_lo_ref, wu_hi_ref, wd_lo_ref, wd_hi_ref, o_ref):
    j = pl.program_id(0)

    @pl.when(valid_ref[j] == 1)
    def _():
        x = xs_ref[:, :D_MODEL]
        w_lo = xs_ref[:, D_MODEL + 1:D_MODEL + 2]
        w_hi = xs_ref[:, D_MODEL + 2:D_MODEL + 3]
        h = _rms(x, g_ref[...]).astype(BF16)
        hid_lo = (_silu(_dot(h, wg_lo_ref[...])) * _dot(h, wu_lo_ref[...]) * w_lo).astype(BF16)
        hid_hi = (_silu(_dot(h, wg_hi_ref[...])) * _dot(h, wu_hi_ref[...]) * w_hi).astype(BF16)
        o_ref[...] = x + (_dot(hid_lo, wd_lo_ref[...]) + _dot(hid_hi, wd_hi_ref[...]))

    @pl.when(valid_ref[j] == 0)
    def _():
        o_ref[...] = jnp.zeros(o_ref.shape, F32)


def _moe(xs, g_ffn, wg, wu, wd, in_blk, e_lo, e_hi, valid):
    tm = TOKEN_TILE
    n_tiles = xs.shape[0] // tm
    lo = lambda j, blk, elo, ehi, va: (elo[j], 0, 0)
    hi = lambda j, blk, elo, ehi, va: (ehi[j], 0, 0)
    up = pl.BlockSpec((None, D_MODEL, D_EXPERT), lo), pl.BlockSpec((None, D_MODEL, D_EXPERT), hi)
    down = pl.BlockSpec((None, D_EXPERT, D_MODEL), lo), pl.BlockSpec((None, D_EXPERT, D_MODEL), hi)
    grid_spec = pltpu.PrefetchScalarGridSpec(
        num_scalar_prefetch=4,
        grid=(n_tiles,),
        in_specs=[pl.BlockSpec((tm, ROW_W), lambda j, blk, elo, ehi, va: (blk[j], 0)),
                  pl.BlockSpec((1, D_MODEL), lambda j, blk, elo, ehi, va: (0, 0)),
                  up[0], up[1], up[0], up[1], down[0], down[1]],
        out_specs=pl.BlockSpec((tm, D_MODEL), lambda j, blk, elo, ehi, va: (j, 0)),
    )
    return pl.pallas_call(
        _moe_kernel,
        grid_spec=grid_spec,
        out_shape=jax.ShapeDtypeStruct((xs.shape[0], D_MODEL), F32),
        compiler_params=_cparams(1),
        name="moe",
    )(in_blk, e_lo, e_hi, valid, xs, g_ffn, wg, wg, wu, wu, wd, wd)


def _ple_kernel(pos_ref, pos_next_ref, xs_hbm, p_ref, gple_ref, wpg_ref, wp_ref, gfin_ref, y_ref, buf, sem):
    i = pl.program_id(0)
    tm = buf.shape[1]
    slot = i % 2

    def gather(idx_ref, s):
        for r in range(tm):
            pltpu.make_async_copy(xs_hbm.at[pl.ds(idx_ref[0, r], 1)], buf.at[s, pl.ds(r, 1)],
                                  sem.at[s]).start(priority=r % 2)

    @pl.when(i == 0)
    def _():
        gather(pos_ref, 0)

    @pl.when(i + 1 < pl.num_programs(0))
    def _():
        gather(pos_next_ref, 1 - slot)

    pltpu.make_async_copy(xs_hbm.at[pl.ds(0, tm)], buf.at[slot], sem.at[slot]).wait()

    x2 = buf[slot]
    hp = _rms(x2, gple_ref[...]).astype(BF16)
    gate = jax.nn.sigmoid(_dot(hp, wpg_ref[...]))
    x3 = x2 + gate * _dot(p_ref[...].astype(BF16), wp_ref[...])
    y_ref[...] = _rms(x3, gfin_ref[...])


def _ple(xs2, pos2d, p, g_ple, wpg, wp, g_final):
    n = p.shape[0]
    tm = TOKEN_TILE
    row = lambda i: (i, 0)
    fixed = lambda i: (0, 0)
    last = n // tm - 1
    return pl.pallas_call(
        _ple_kernel,
        grid=(n // tm,),
        in_specs=[pl.BlockSpec((None, 1, tm), lambda i: (i, 0, 0), memory_space=pltpu.SMEM),
                  pl.BlockSpec((None, 1, tm), lambda i: (jnp.minimum(i + 1, last), 0, 0), memory_space=pltpu.SMEM),
                  pl.BlockSpec(memory_space=pl.ANY),
                  pl.BlockSpec((tm, P_DIM), row), pl.BlockSpec((1, D_MODEL), fixed),
                  pl.BlockSpec((D_MODEL, D_MODEL), fixed), pl.BlockSpec((P_DIM, D_MODEL), fixed),
                  pl.BlockSpec((1, D_MODEL), fixed)],
        out_specs=pl.BlockSpec((tm, D_MODEL), row),
        out_shape=jax.ShapeDtypeStruct((n, D_MODEL), F32),
        scratch_shapes=[pltpu.VMEM((2, tm, D_MODEL), F32), pltpu.SemaphoreType.DMA((2,))],
        compiler_params=_cparams(1),
        name="ple",
    )(pos2d, pos2d, xs2, p, g_ple, wpg, wp, g_final)


def _routing_plan(xr, counts, n):
    tm = TOKEN_TILE
    n_tiles = n // tm + N_CLASSES
    cls = xr[:, D_MODEL].astype(I32)
    rank = xr[:, D_MODEL + 3].astype(I32)
    cnt = counts[0, :N_CLASSES].astype(I32)
    tiles = (cnt + tm - 1) // tm
    tile_end = jnp.cumsum(tiles)
    tile_start = tile_end - tiles
    n_used = tile_end[-1]
    pos = (tile_start * tm)[cls] + rank
    j = jnp.arange(n_tiles, dtype=I32)
    valid = j < n_used
    in_blk = jnp.minimum(j, n_used - 1)
    tcls = jnp.sum((tile_end[None, :] <= in_blk[:, None]).astype(I32), axis=1)
    grp = tcls // N_PAIRS
    pair = tcls % N_PAIRS
    e_lo = grp * EXPERTS_PER_GROUP + jnp.asarray(PAIR_LO, I32)[pair]
    e_hi = grp * EXPERTS_PER_GROUP + jnp.asarray(PAIR_HI, I32)[pair]
    seg_zero = jnp.where(tiles > 0, tile_end * tm - tm, -1)
    tail = n_used + jnp.arange(N_CLASSES, dtype=I32)
    tail_zero = jnp.where(tail < n_tiles, tail * tm, -1)
    zero_starts = jnp.concatenate([seg_zero, tail_zero]).astype(I32)
    return pos.reshape(n // tm, 1, tm), zero_starts, in_blk, e_lo, e_hi, valid.astype(I32), n_tiles * tm


def _rope_tables(pos, rows):
    half = DH_A // 2
    inv_freq = ROPE_THETA ** (-jnp.arange(half, dtype=F32) / half)
    ang = pos.astype(F32)[:, None] * inv_freq[None, :]
    cos = jnp.cos(ang)
    sin = jnp.sin(ang)
    cos_t = jnp.tile(jnp.concatenate([cos, cos], axis=-1), (1, ATT_W // DH_A))
    sin_t = jnp.tile(jnp.concatenate([-sin, sin], axis=-1), (1, ATT_W // DH_A))
    reps = max(rows // pos.shape[0], 1)
    return jnp.tile(cos_t, (reps, 1)), jnp.tile(sin_t, (reps, 1))


def _layer(i, x, p, pos, attend, s0, w, batch, seq, transposed):
    n = batch * seq
    small = BF16 if seq % 16 == 0 else F32
    cos_t, sin_t = _rope_tables(pos, TOKEN_TILE)
    proj = _inproj(x, w["g_mix"], w["w_seq"], cos_t, sin_t, batch, seq, transposed)
    lam_init = 0.8 - 0.6 * math.exp(-0.3 * i)
    ya, kf, vf = attend(proj, lam_init)
    qr, fl, ir, gr = proj[-4:]
    yr, s_new = _hgrn(qr, fl, ir, gr, w["lb"], w["g_rec"], s0, batch, seq, small)
    xr, counts = _mix(x, ya.astype(BF16), yr.astype(BF16), w["g_mix"], w["w_gate_a"], w["w_gate_r"], w["w_branch_a"],
                      w["w_branch_r"], w["w_out"], w["g_ffn"], w["w_route"], w["b_route"])
    pos2d, zero_starts, in_blk, e_lo, e_hi, valid, n_sorted = _routing_plan(xr, counts, n)
    xs = _dispatch(xr, pos2d, zero_starts, n_sorted)
    xs2 = _moe(xs, w["g_ffn"], w["w_exp_gate"], w["w_exp_up"], w["w_exp_down"], in_blk, e_lo, e_hi, valid)
    y = _ple(xs2, pos2d, p, w["g_ple"], w["w_ple_gate"], w["w_ple"], w["g_final"])
    return y, kf, vf, s_new


def kernel(x_prompt, x_sample, p_prompt, p_sample, cache_k, cache_v, state_hgrn, page_table, g_mix, w_in, lam,
           g_subln, lb_param, g_rec, w_branch_a, w_branch_r, w_out, g_ffn, w_route_group, b_route_group,
           w_route_expert, b_route_expert, w_exp_gate, w_exp_up, w_exp_down, g_ple, w_ple_gate, w_ple, g_final):
    depth = w_in.shape[0]
    assert depth == 1, "single-layer step"
    bp, sp, _ = x_prompt.shape
    bs, ts, _ = x_sample.shape
    past_len = page_table.shape[1] * PAGE_SIZE
    i = 0

    w_in_b = w_in[i].astype(BF16)
    n_route = N_GROUPS + N_EXPERTS
    w_route = jnp.concatenate([w_route_group[i], w_route_expert[i]], axis=1)
    w_route = jnp.pad(w_route, ((0, 0), (0, LANES - n_route))).astype(BF16)
    b_route = jnp.pad(jnp.concatenate([b_route_group[i], b_route_expert[i]]), (0, LANES - n_route)).reshape(1, LANES)
    lb = jnp.cumsum(jax.nn.softmax(lb_param.astype(F32), axis=0), axis=0)[i].reshape(1, REC_W)
    w = dict(
        g_mix=g_mix[i].reshape(1, D_MODEL), w_seq=w_in_b[:, :N_SEQ_COLS],
        w_gate_a=w_in_b[:, N_SEQ_COLS:N_SEQ_COLS + D_MODEL], w_gate_r=w_in_b[:, N_SEQ_COLS + D_MODEL:],
        lb=lb, g_rec=g_rec[i].reshape(1, V_R),
        w_branch_a=w_branch_a[i].astype(BF16), w_branch_r=w_branch_r[i].astype(BF16), w_out=w_out[i].astype(BF16),
        g_ffn=g_ffn[i].reshape(1, D_MODEL), w_route=w_route, b_route=b_route,
        w_exp_gate=w_exp_gate[i].astype(BF16), w_exp_up=w_exp_up[i].astype(BF16), w_exp_down=w_exp_down[i].astype(BF16),
        g_ple=g_ple[i].reshape(1, D_MODEL), w_ple_gate=w_ple_gate[i].astype(BF16), w_ple=w_ple[i].astype(BF16),
        g_final=g_final.reshape(1, D_MODEL),
    )
    lam_i = lam[i].astype(F32)
    g_sub = g_subln[i].reshape(1, E_A)

    def attend_prompt(proj, lam_init):
        q_t, k_t, kb, vf, v_t = proj[:5]
        return _attn_prompt(q_t, kb, v_t, lam_i, g_sub.reshape(E_A, 1), bp, sp, lam_init), k_t, vf

    n_phys = cache_k.shape[1]
    ckt = jnp.transpose(cache_k[i], (0, 2, 3, 4, 1)).reshape(n_phys, ATT_W, PAGE_SIZE)
    cv2 = cache_v[i].reshape(n_phys, PAGE_SIZE * H_A, E_A)

    def attend_sample(proj, lam_init):
        q, kf, vf = proj[:3]
        return _attn_sample(q, kf, vf, ckt, cv2, page_table, lam_i, g_sub, bs, ts, lam_init), kf, vf

    pos_p = jnp.arange(sp, dtype=I32)
    pos_s = past_len + jnp.arange(ts, dtype=I32)
    s0_p = jnp.zeros((bp, H_R, K_R, V_R), F32)

    y_p, k_t, v_p, s_p = _layer(i, x_prompt.reshape(bp * sp, D_MODEL), p_prompt[i].reshape(bp * sp, P_DIM),
                                pos_p, attend_prompt, s0_p, w, bp, sp, True)
    y_s, k_s, v_s, s_s = _layer(i, x_sample.reshape(bs * ts, D_MODEL), p_sample[i].reshape(bs * ts, P_DIM),
                                pos_s, attend_sample, state_hgrn[i], w, bs, ts, False)
    k_p = jnp.transpose(k_t.reshape(1, bp, H_A, 2, DH_A, sp), (0, 1, 5, 2, 3, 4))

    return (y_p.reshape(bp, sp, D_MODEL), y_s.reshape(bs, ts, D_MODEL),
            k_p, v_p.reshape(1, bp, sp, H_A, E_A), s_p.reshape(1, bp, H_R, K_R, V_R),
            k_s.reshape(1, bs, ts, H_A, 2, DH_A), v_s.reshape(1, bs, ts, H_A, E_A), s_s.reshape(1, bs, H_R, K_R, V_R))
```

```python
import functools
import math

import jax
import jax.numpy as jnp
import numpy as np
from jax import lax
from jax.experimental import pallas as pl
from jax.experimental.pallas import tpu as pltpu

F32 = jnp.float32
BF16 = jnp.bfloat16
I32 = jnp.int32

D_MODEL = 1024
P_DIM = 256
H_A = 4
DH_A = 64
E_A = 2 * DH_A
H_R = 4
K_R = 128
V_R = 128
HGRN_CHUNK = 32
N_GROUPS = 4
EXPERTS_PER_GROUP = 4
N_EXPERTS = N_GROUPS * EXPERTS_PER_GROUP
D_EXPERT = 256
ROPE_THETA = 10000.0
EPS = 1e-6
NEG_INF = -1e30
LOG2_E = 1.4426950408889634
PAGE_SIZE = 128
ATT_W = H_A * E_A
REC_W = H_R * V_R
N_SEQ_COLS = 7 * 512
LANES = 128
N_PAIRS = 6
N_CLASSES = N_GROUPS * N_PAIRS
PAIR_LO = (0, 0, 0, 1, 1, 2)
PAIR_HI = (1, 2, 3, 2, 3, 3)
ROW_W = D_MODEL + LANES
TOKEN_TILE = 256
MIX_TILE = 512
MIX_PART = 256
ATTN_TILE = 512
QUERY_GROUP = 256
SCORE_LOOKAHEAD = 4
PAGES_PER_STEP = 16
VMEM_LIMIT = 56 * 1024 * 1024


def _cparams(n_axes, flags=None):
    return pltpu.CompilerParams(dimension_semantics=("arbitrary",) * n_axes, vmem_limit_bytes=VMEM_LIMIT, flags=flags)


def _rms(x, g):
    return x * lax.rsqrt(jnp.mean(x * x, axis=-1, keepdims=True) + EPS) * g


def _silu(x):
    return x * jax.nn.sigmoid(x)


def _dot(a, b):
    return jnp.dot(a, b, preferred_element_type=F32)


def _dot_nt(a, b):
    return lax.dot_general(a, b, (((1,), (1,)), ((), ())), preferred_element_type=F32)


def _inproj_kernel(x_ref, g_ref, w_ref, cos_ref, sin_ref, *out_refs, transposed):
    h = _rms(x_ref[...], g_ref[...]).astype(BF16)
    cos = cos_ref[...]
    sin = sin_ref[...]
    tm = h.shape[0]
    lane = lax.broadcasted_iota(I32, (tm, ATT_W), 1)
    first_half = (lane % DH_A) < (DH_A // 2)

    def proj(c):
        return _dot(h, w_ref[:, c * 512:(c + 1) * 512])

    def rope(t):
        rot = jnp.where(first_half, pltpu.roll(t, ATT_W - DH_A // 2, 1), pltpu.roll(t, DH_A // 2, 1))
        return t * cos + rot * sin

    q = rope(proj(0)) * (DH_A ** -0.5)
    k = rope(proj(1))
    v = proj(2)
    if transposed:
        qt_ref, kt_ref, kb_ref, vf_ref, vt_ref = out_refs[:5]
        qt_ref[...] = (q * LOG2_E).T.astype(BF16)
        kt_ref[...] = k.T
        kb_ref[...] = k.astype(BF16)
        for head in range(H_A):
            vf_ref[pl.ds(head, tm, stride=H_A), :] = v[:, head * E_A:(head + 1) * E_A]
        vt_ref[...] = v.T.astype(BF16)
        rest = out_refs[5:]
    else:
        q_ref, kf_ref, vf_ref = out_refs[:3]
        q_ref[...] = q
        kf_ref[...] = k
        vf_ref[...] = v
        rest = out_refs[3:]
    qr_ref, fl_ref, ir_ref, gr_ref = rest
    qr_ref[...] = _silu(proj(3))
    fl_ref[...] = proj(4)
    ir_ref[...] = proj(5)
    gr_ref[...] = proj(6)


def _inproj(x, g_mix, w_seq, cos_t, sin_t, batch, seq, transposed):
    n = x.shape[0]
    tm = TOKEN_TILE
    nblk = max(seq // tm, 1)
    row = lambda i: (i, 0)
    tab = lambda i: (i % nblk, 0)
    const = lambda i: (0, 0)
    o512 = pl.BlockSpec((tm, 512), row)
    sds = lambda dt: jax.ShapeDtypeStruct((n, 512), dt)
    if transposed:
        ot = pl.BlockSpec((None, 512, tm), lambda i: (i // nblk, 0, i % nblk))
        sdt = lambda dt: jax.ShapeDtypeStruct((batch, 512, seq), dt)
        ov = pl.BlockSpec((tm * H_A, E_A), row)
        out_specs = [ot, ot, o512, ov, ot] + [o512] * 4
        out_shape = ([sdt(BF16), sdt(F32), sds(BF16), jax.ShapeDtypeStruct((n * H_A, E_A), F32), sdt(BF16)]
                     + [sds(F32)] * 4)
    else:
        out_specs = [o512] * 7
        out_shape = [sds(F32)] * 7
    return pl.pallas_call(
        functools.partial(_inproj_kernel, transposed=transposed),
        grid=(n // tm,),
        in_specs=[pl.BlockSpec((tm, D_MODEL), row), pl.BlockSpec((1, D_MODEL), const),
                  pl.BlockSpec((D_MODEL, N_SEQ_COLS), const),
                  pl.BlockSpec((tm, ATT_W), tab), pl.BlockSpec((tm, ATT_W), tab)],
        out_specs=out_specs,
        out_shape=out_shape,
        compiler_params=_cparams(1),
        name="inproj",
    )(x, g_mix, w_seq, cos_t, sin_t)


def _lambda_value(lam_ref):
    lam = lam_ref[...]
    s01 = jnp.sum(lam[0:1, :] * lam[1:2, :], axis=-1, keepdims=True)
    s23 = jnp.sum(lam[2:3, :] * lam[3:4, :], axis=-1, keepdims=True)
    return jnp.exp(s01) - jnp.exp(s23)


def _diff_norm(o0, o1, lam_val, g_sub, lam_init):
    od = o0 - lam_val * o1
    return _rms(od, g_sub) * (1.0 - lam_init)


def _split_maps(qh):
    lane = lax.broadcasted_iota(I32, qh.shape, 1)
    zero = jnp.zeros_like(qh)
    return jnp.concatenate([jnp.where(lane < DH_A, qh, zero), jnp.where(lane >= DH_A, qh, zero)], axis=0)


def _attn_prompt_kernel(q_ref, k_ref, v_ref, lam_ref, g_ref, o_ref, q2_scr, *state, lam_init):
    i = pl.program_id(1)
    tq = q_ref.shape[1]
    chains = [(h, c0) for h in range(H_A) for c0 in range(0, 2 * tq, QUERY_GROUP)]
    m_scrs, l_scrs, acc_scrs = state[0::3], state[1::3], state[2::3]
    feat = lax.broadcasted_iota(I32, (E_A, tq), 0)
    for h in range(H_A):
        qh = q_ref[h * E_A:(h + 1) * E_A, :].astype(F32)
        q2_scr[h] = jnp.concatenate([jnp.where(feat < DH_A, qh, 0.0), jnp.where(feat >= DH_A, qh, 0.0)],
                                    axis=1).astype(BF16)
    for n in range(len(chains)):
        m_scrs[n][...] = jnp.full(m_scrs[n].shape, NEG_INF, F32)
        l_scrs[n][...] = jnp.zeros(l_scrs[n].shape, F32)
        acc_scrs[n][...] = jnp.zeros(acc_scrs[n].shape, F32)

    def key_block(kb, on_diagonal):
        r0 = pl.multiple_of(kb * tq, tq)

        def scores(n):
            h, c0 = chains[n]
            s = _dot(k_ref[pl.ds(r0, tq), h * E_A:(h + 1) * E_A], q2_scr[h, :, c0:c0 + QUERY_GROUP])
            if on_diagonal:
                key = lax.broadcasted_iota(I32, (tq, QUERY_GROUP), 0)
                qry = (lax.broadcasted_iota(I32, (tq, QUERY_GROUP), 1) + c0) % tq
                s = jnp.where(key > qry, NEG_INF, s)
            return s

        ahead = [scores(n) for n in range(min(SCORE_LOOKAHEAD, len(chains)))]
        for n, (h, c0) in enumerate(chains):
            cs = slice(h * E_A, (h + 1) * E_A)
            s = ahead.pop(0)
            if n + SCORE_LOOKAHEAD < len(chains):
                ahead.append(scores(n + SCORE_LOOKAHEAD))
            m_old = m_scrs[n][...]
            m_new = jnp.maximum(m_old, jnp.max(s, axis=0, keepdims=True))
            alpha = jnp.exp2(m_old - m_new)
            p = jnp.exp2(s - m_new)
            l_scrs[n][...] = l_scrs[n][...] * alpha + jnp.sum(p, axis=0, keepdims=True)
            acc_scrs[n][...] = acc_scrs[n][...] * alpha + _dot(v_ref[cs, pl.ds(r0, tq)], p.astype(BF16))
            m_scrs[n][...] = m_new

    def below_diagonal(kb, carry):
        key_block(kb, False)
        return carry

    lax.fori_loop(0, i, below_diagonal, 0)
    key_block(i, True)

    lam_val = _lambda_value(lam_ref) + lam_init
    per_head = len(chains) // H_A
    for h in range(H_A):
        o = jnp.concatenate([acc_scrs[n][...] / l_scrs[n][...] for n in range(h * per_head, (h + 1) * per_head)],
                            axis=1)
        od = o[:, :tq] - lam_val * o[:, tq:]
        ms = jnp.mean(od * od, axis=0, keepdims=True)
        y_t = od * lax.rsqrt(ms + EPS) * g_ref[...] * (1.0 - lam_init)
        o_ref[:, h * E_A:(h + 1) * E_A] = y_t.T.astype(o_ref.dtype)


def _attn_prompt(q_t, k, v_t, lam, g_sub_col, batch, seq, lam_init):
    tq = min(ATTN_TILE, seq)
    nq = seq // tq
    n_chains = H_A * (2 * tq // QUERY_GROUP)
    state = [pltpu.VMEM((1, QUERY_GROUP), F32), pltpu.VMEM((1, QUERY_GROUP), F32),
             pltpu.VMEM((E_A, QUERY_GROUP), F32)] * n_chains
    return pl.pallas_call(
        functools.partial(_attn_prompt_kernel, lam_init=lam_init),
        grid=(batch, nq),
        in_specs=[pl.BlockSpec((None, ATT_W, tq), lambda b, i: (b, 0, i)),
                  pl.BlockSpec((seq, ATT_W), lambda b, i: (b, 0)),
                  pl.BlockSpec((None, ATT_W, seq), lambda b, i: (b, 0, 0)),
                  pl.BlockSpec((4, DH_A), lambda b, i: (0, 0)),
                  pl.BlockSpec((E_A, 1), lambda b, i: (0, 0))],
        out_specs=pl.BlockSpec((tq, ATT_W), lambda b, i: (b * nq + i, 0)),
        out_shape=jax.ShapeDtypeStruct(k.shape, BF16),
        scratch_shapes=[pltpu.VMEM((H_A, E_A, 2 * tq), BF16)] + state,
        compiler_params=_cparams(2),
        name="attn_prompt",
    )(q_t, k, v_t, lam, g_sub_col)


def _attn_sample_kernel(pt_ref, q_ref, kn_ref, vn_ref, lam_ref, g_ref, *refs, lam_init, pages):
    k_refs = refs[:pages]
    v_refs = refs[pages:2 * pages]
    o_ref = refs[2 * pages]
    m_scr, l_scr, acc_scr = refs[2 * pages + 1:]
    j = pl.program_id(1)
    t = q_ref.shape[0]

    @pl.when(j == 0)
    def _():
        m_scr[...] = jnp.full(m_scr.shape, NEG_INF, F32)
        l_scr[...] = jnp.zeros(l_scr.shape, F32)
        acc_scr[...] = jnp.zeros(acc_scr.shape, F32)

    rows_h = 2 * t

    def own_head_blocks(big):
        return jnp.concatenate([big[h * rows_h:(h + 1) * rows_h, h * E_A:(h + 1) * E_A] for h in range(H_A)], axis=0)

    def update(s, v_all):
        m_old = m_scr[...]
        m_new = jnp.maximum(m_old, jnp.max(s, axis=-1, keepdims=True))
        alpha = jnp.exp(m_old - m_new)
        p = jnp.exp(s - m_new)
        l_scr[...] = l_scr[...] * alpha + jnp.sum(p, axis=-1, keepdims=True)
        acc_scr[...] = acc_scr[...] * alpha + own_head_blocks(_dot(p.astype(BF16), v_all))
        m_scr[...] = m_new

    q = q_ref[...]
    zero = jnp.zeros((rows_h, E_A), F32)
    qall = jnp.concatenate(
        [jnp.concatenate([_split_maps(q[:, h * E_A:(h + 1) * E_A]) if hh == h else zero for hh in range(H_A)], axis=1)
         for h in range(H_A)], axis=0).astype(BF16)

    kt = jnp.concatenate([k_refs[r][...].astype(BF16) for r in range(pages)], axis=1)
    v_all = jnp.concatenate(
        [jnp.concatenate([v_refs[r][pl.ds(h, PAGE_SIZE, stride=H_A), :] for h in range(H_A)], axis=1)
         for r in range(pages)], axis=0).astype(BF16)
    update(_dot(qall, kt), v_all)

    @pl.when(j == pl.num_programs(1) - 1)
    def _():
        lam_val = _lambda_value(lam_ref) + lam_init
        n_rows = H_A * rows_h
        row_t = lax.broadcasted_iota(I32, (n_rows, PAGE_SIZE), 0) % t
        col = lax.broadcasted_iota(I32, (n_rows, PAGE_SIZE), 1)
        visible = col <= row_t
        pad = jnp.zeros((PAGE_SIZE - t, ATT_W), F32)
        kn_t = jnp.concatenate([kn_ref[...], pad], axis=0).T.astype(BF16)
        vn = jnp.concatenate([vn_ref[...], pad], axis=0).astype(BF16)
        update(jnp.where(visible, _dot(qall, kn_t), NEG_INF), vn)
        o = acc_scr[...] / l_scr[...]
        for h in range(H_A):
            o0 = o[h * rows_h:h * rows_h + t]
            o1 = o[h * rows_h + t:(h + 1) * rows_h]
            y = _diff_norm(o0, o1, lam_val, g_ref[...], lam_init)
            o_ref[:, h * E_A:(h + 1) * E_A] = y.astype(BF16).astype(o_ref.dtype)


def _attn_sample(q, kn, vn, cache_kt, cache_v2, page_table, lam, g_sub, batch, t, lam_init):
    pages = PAGES_PER_STEP
    n_pages = page_table.shape[1]
    steps = n_pages // pages
    fixed = lambda b, j, pt: (0, 0)
    rowb = lambda b, j, pt: (b, 0)
    n_rows = H_A * 2 * t

    def page_spec(r):
        return pl.BlockSpec((None, ATT_W, PAGE_SIZE), lambda b, j, pt: (pt[b, j * pages + r], 0, 0))

    grid_spec = pltpu.PrefetchScalarGridSpec(
        num_scalar_prefetch=1,
        grid=(batch, steps),
        in_specs=[pl.BlockSpec((t, ATT_W), rowb), pl.BlockSpec((t, ATT_W), rowb), pl.BlockSpec((t, ATT_W), rowb),
                  pl.BlockSpec((4, DH_A), fixed), pl.BlockSpec((1, E_A), fixed)]
                 + [page_spec(r) for r in range(pages)] + [page_spec(r) for r in range(pages)],
        out_specs=pl.BlockSpec((t, ATT_W), rowb),
        scratch_shapes=[pltpu.VMEM((n_rows, 1), F32), pltpu.VMEM((n_rows, 1), F32), pltpu.VMEM((n_rows, E_A), F32)],
    )
    return pl.pallas_call(
        functools.partial(_attn_sample_kernel, lam_init=lam_init, pages=pages),
        grid_spec=grid_spec,
        out_shape=jax.ShapeDtypeStruct(q.shape, F32),
        compiler_params=_cparams(2),
        name="attn_sample",
    )(page_table, q, kn, vn, lam, g_sub, *([cache_kt] * pages), *([cache_v2] * pages))


def _split3(x):
    a = x.astype(BF16)
    r = x - a.astype(F32)
    b = r.astype(BF16)
    c = (r - b.astype(F32)).astype(BF16)
    return a, b, c


def _hgrn_kernel(qr_ref, fl_ref, ir_ref, gr_ref, lb_ref, g_ref, s0_ref, y_ref, s_ref,
                 qe_scr, kk_scr, dl_scr, o_scr, st_scr, *, chunk, blk):
    t = qr_ref.shape[0]
    lb = lb_ref[...]
    row = lax.broadcasted_iota(I32, (blk, blk), 0)
    col = lax.broadcasted_iota(I32, (blk, blk), 1)
    same = (row // chunk) == (col // chunk)
    causal = jnp.logical_and(same, col <= row)
    cum_mask = jnp.where(causal, 1.0, 0.0).astype(BF16)
    all_mask = jnp.where(same, 1.0, 0.0).astype(BF16)

    def gates(b, carry):
        rs = pl.ds(pl.multiple_of(b * blk, blk), blk)
        f = lb + (1.0 - lb) * jax.nn.sigmoid(fl_ref[rs, :])
        g = jnp.log(f)
        k = 1.0 - f
        g1, g2, g3 = _split3(g)
        bcum = _dot(cum_mask, g1) + _dot(cum_mask, g2) + _dot(cum_mask, g3)
        blast = _dot(all_mask, g1) + _dot(all_mask, g2) + _dot(all_mask, g3)
        qe = (qr_ref[rs, :] * jnp.exp(bcum)).astype(BF16)
        kd = (k * jnp.exp(-bcum)).astype(BF16)
        qe_scr[rs, :] = qe.astype(qe_scr.dtype)
        kk_scr[rs, :] = (k * jnp.exp(blast - bcum)).astype(kk_scr.dtype)
        dl_scr[rs, :] = jnp.exp(blast)
        v = ir_ref[rs, :].astype(BF16)
        heads = [slice(h * K_R, (h + 1) * K_R) for h in range(H_R)]
        a = [jnp.where(causal, _dot_nt(qe[:, cs], kd[:, cs]), 0.0).astype(BF16) for cs in heads]
        for cs, a_h in zip(heads, a):
            o_scr[rs, cs] = _dot(a_h, v[:, cs])
        return carry

    lax.fori_loop(0, t // blk, gates, 0, unroll=2 if t // blk >= 2 else 1)

    for h in range(H_R):
        st_scr[h] = s0_ref[h].T

    def step(j, carry):
        r0 = pl.multiple_of(j * chunk, chunk)
        rows = pl.ds(r0, chunk)
        for h in range(H_R):
            cs = slice(h * K_R, (h + 1) * K_R)
            st = st_scr[h]
            o_scr[rows, cs] = o_scr[rows, cs] + _dot_nt(qe_scr[rows, cs].astype(BF16), st.astype(BF16))
            ut = _dot(ir_ref[rows, cs].T.astype(BF16), kk_scr[rows, cs].astype(BF16))
            st_scr[h] = st * dl_scr[pl.ds(r0, 1), cs] + ut
        return carry

    lax.fori_loop(0, t // chunk, step, 0, unroll=2 if t // chunk >= 2 else 1)
    for h in range(H_R):
        s_ref[h] = st_scr[h].T

    g_rec = g_ref[...]

    def finish(b, carry):
        rs = pl.ds(pl.multiple_of(b * blk, blk), blk)
        for h in range(H_R):
            cs = slice(h * K_R, (h + 1) * K_R)
            y = _rms(o_scr[rs, cs], g_rec) * _silu(gr_ref[rs, cs])
            y_ref[rs, cs] = y.astype(BF16).astype(y_ref.dtype)
        return carry

    lax.fori_loop(0, t // blk, finish, 0)


def _hgrn(qr, fl, ir, gr, lb, g_rec, s0, batch, t, out_dtype):
    chunk = math.gcd(t, HGRN_CHUNK)
    blk = min(t, 256)
    rowb = lambda b: (b, 0)
    fixed = lambda b: (0, 0)
    seq = pl.BlockSpec((t, REC_W), rowb)
    state = pl.BlockSpec((None, H_R, K_R, V_R), lambda b: (b, 0, 0, 0))
    return pl.pallas_call(
        functools.partial(_hgrn_kernel, chunk=chunk, blk=blk),
        grid=(batch,),
        in_specs=[seq, seq, seq, seq, pl.BlockSpec((1, REC_W), fixed), pl.BlockSpec((1, V_R), fixed), state],
        out_specs=[seq, state],
        out_shape=[jax.ShapeDtypeStruct(qr.shape, out_dtype), jax.ShapeDtypeStruct(s0.shape, F32)],
        scratch_shapes=[pltpu.VMEM((t, REC_W), out_dtype), pltpu.VMEM((t, REC_W), out_dtype),
                        pltpu.VMEM((t, REC_W), F32), pltpu.VMEM((t, REC_W), F32), pltpu.VMEM((H_R, V_R, K_R), F32)],
        compiler_params=_cparams(1),
        name="hgrn",
    )(qr, fl, ir, gr, lb, g_rec, s0)


def _mix_kernel(x_ref, ya_ref, yr_ref, gmix_ref, wga_ref, wgr_ref, wba_ref, wbr_ref, wout_ref,
                gffn_ref, wrt_ref, brt_ref, xr_ref, cnt_ref, cnt_scr):
    i = pl.program_id(0)
    tm = x_ref.shape[0]

    @pl.when(i == 0)
    def _():
        cnt_scr[...] = jnp.zeros(cnt_scr.shape, F32)

    part = min(tm, MIX_PART)
    parts = [slice(r, r + part) for r in range(0, tm, part)]
    xs = [x_ref[rs, :] for rs in parts]
    hs = [_rms(x, gmix_ref[...]).astype(BF16) for x in xs]
    gate_a = [jax.nn.sigmoid(_dot(h, wga_ref[...])) for h in hs]
    br_a = [_dot(ya_ref[rs, :], wba_ref[...]) for rs in parts]
    gate_r = [jax.nn.sigmoid(_dot(h, wgr_ref[...])) for h in hs]
    br_r = [_dot(yr_ref[rs, :], wbr_ref[...]) for rs in parts]
    merged = [(ga * ba + gr * br).astype(BF16) for ga, ba, gr, br in zip(gate_a, br_a, gate_r, br_r)]
    x1s = [x + _dot(m, wout_ref[...]) for x, m in zip(xs, merged)]
    for rs, x1 in zip(parts, x1s):
        xr_ref[rs, :D_MODEL] = x1
    hfs = [_rms(x1, gffn_ref[...]).astype(BF16) for x1 in x1s]
    lgs = [_dot(hf, wrt_ref[...]) + brt_ref[...] for hf in hfs]
    for rs, lg in zip(parts, lgs):
        xr_ref[rs, D_MODEL:] = _route_rows(lg, cnt_scr)
    cnt_ref[...] = cnt_scr[...]


def _route_rows(lg, cnt_scr):
    tm = lg.shape[0]
    lane = lax.broadcasted_iota(I32, (tm, LANES), 1)
    is_group = lane < N_GROUPS
    mg = jnp.max(jnp.where(is_group, lg, NEG_INF), axis=-1, keepdims=True)
    gidx = jnp.min(jnp.where(jnp.logical_and(is_group, lg == mg), lane, LANES), axis=-1, keepdims=True)
    p_top = 1.0 / jnp.sum(jnp.where(is_group, jnp.exp(lg - mg), 0.0), axis=-1, keepdims=True)
    base = N_GROUPS + EXPERTS_PER_GROUP * gidx
    in_grp = jnp.logical_and(lane >= base, lane < base + EXPERTS_PER_GROUP)
    v1 = jnp.max(jnp.where(in_grp, lg, NEG_INF), axis=-1, keepdims=True)
    e1 = jnp.min(jnp.where(jnp.logical_and(in_grp, lg == v1), lane, LANES), axis=-1, keepdims=True)
    rest = jnp.logical_and(in_grp, lane != e1)
    v2 = jnp.max(jnp.where(rest, lg, NEG_INF), axis=-1, keepdims=True)
    e2 = jnp.min(jnp.where(jnp.logical_and(rest, lg == v2), lane, LANES), axis=-1, keepdims=True)
    tt = jnp.exp(v2 - v1)
    w_a = (1.0 / (1.0 + tt)) * p_top
    w_b = (tt / (1.0 + tt)) * p_top
    a = e1 - base
    b = e2 - base
    a_first = a < b
    lo = jnp.minimum(a, b)
    hi = jnp.maximum(a, b)
    w_lo = jnp.where(a_first, w_a, w_b)
    w_hi = jnp.where(a_first, w_b, w_a)
    pair = jnp.where(lo == 0, hi - 1, jnp.where(lo == 1, hi + 1, 5))
    cls = gidx * N_PAIRS + pair

    onehot = lane == cls
    r_i = lax.broadcasted_iota(I32, (tm, tm), 0)
    c_i = lax.broadcasted_iota(I32, (tm, tm), 1)
    before = jnp.where(c_i < r_i, 1.0, 0.0).astype(BF16)
    excl = _dot(before, jnp.where(onehot, 1.0, 0.0).astype(BF16))
    rank = jnp.sum(jnp.where(onehot, excl + cnt_scr[...], 0.0), axis=-1, keepdims=True)
    cnt_scr[...] = cnt_scr[...] + jnp.sum(jnp.where(onehot, 1.0, 0.0), axis=0, keepdims=True)
    return jnp.where(lane == 0, cls.astype(F32),
                     jnp.where(lane == 1, w_lo, jnp.where(lane == 2, w_hi, jnp.where(lane == 3, rank, 0.0))))


def _mix(x, ya, yr, g_mix, wga, wgr, wba, wbr, wout, g_ffn, wrt, brt):
    n = x.shape[0]
    tm = min(MIX_TILE, n)
    row = lambda i: (i, 0)
    fixed = lambda i: (0, 0)
    full = lambda a: pl.BlockSpec(a.shape, fixed)
    return pl.pallas_call(
        _mix_kernel,
        grid=(n // tm,),
        in_specs=[pl.BlockSpec((tm, D_MODEL), row), pl.BlockSpec((tm, ATT_W), row), pl.BlockSpec((tm, REC_W), row),
                  full(g_mix), full(wga), full(wgr), full(wba), full(wbr), full(wout), full(g_ffn), full(wrt), full(brt)],
        out_specs=[pl.BlockSpec((tm, ROW_W), row), pl.BlockSpec((1, LANES), fixed)],
        out_shape=[jax.ShapeDtypeStruct((n, ROW_W), F32), jax.ShapeDtypeStruct((1, LANES), F32)],
        scratch_shapes=[pltpu.VMEM((1, LANES), F32)],
        compiler_params=_cparams(1),
        name="mix",
    )(x, ya, yr, g_mix, wga, wgr, wba, wbr, wout, g_ffn, wrt, brt)


def _row_copy_out(x_ref, o_hbm, sem, r, p):
    return pltpu.make_async_copy(x_ref.at[pl.ds(r, 1)], o_hbm.at[pl.ds(p, 1)], sem)


def _dispatch_kernel(zs_ref, pos_ref, x_ref, o_hbm, zero_scr, zsem, rsem):
    i = pl.program_id(0)
    tm = x_ref.shape[0]

    def zero_copy(c):
        start = pl.multiple_of(jnp.maximum(zs_ref[c], 0), tm)
        return pltpu.make_async_copy(zero_scr, o_hbm.at[pl.ds(start, tm)], zsem)

    @pl.when(i == 0)
    def _():
        zero_scr[...] = jnp.zeros(zero_scr.shape, F32)
        for c in range(zs_ref.shape[0]):
            @pl.when(zs_ref[c] >= 0)
            def _():
                zero_copy(c).start()
        for c in range(zs_ref.shape[0]):
            @pl.when(zs_ref[c] >= 0)
            def _():
                zero_copy(c).wait()

    for r in range(tm):
        _row_copy_out(x_ref, o_hbm, rsem, r, pos_ref[0, r]).start(priority=r % 2)
    pltpu.make_async_copy(x_ref, o_hbm.at[pl.ds(0, tm)], rsem).wait()


def _dispatch(xr, pos2d, zero_starts, n_rows_sorted):
    n = xr.shape[0]
    tm = TOKEN_TILE
    grid_spec = pltpu.PrefetchScalarGridSpec(
        num_scalar_prefetch=1,
        grid=(n // tm,),
        in_specs=[pl.BlockSpec((None, 1, tm), lambda i, zs: (i, 0, 0), memory_space=pltpu.SMEM),
                  pl.BlockSpec((tm, ROW_W), lambda i, zs: (i, 0))],
        out_specs=pl.BlockSpec(memory_space=pl.ANY),
        scratch_shapes=[pltpu.VMEM((tm, ROW_W), F32), pltpu.SemaphoreType.DMA(()), pltpu.SemaphoreType.DMA(())],
    )
    return pl.pallas_call(
        _dispatch_kernel,
        grid_spec=grid_spec,
        out_shape=jax.ShapeDtypeStruct((n_rows_sorted, ROW_W), F32),
        compiler_params=_cparams(1),
        name="dispatch",
    )(zero_starts, pos2d, xr)


def _moe_kernel(blk_ref, elo_ref, ehi_ref, valid_ref, xs_ref, g_ref,
                wg_lo_ref, wg_hi_ref, wu_lo_ref, wu_hi_ref, wd_lo_ref, wd_hi_ref, o_ref):
    j = pl.program_id(0)

    @pl.when(valid_ref[j] == 1)
    def _():
        x = xs_ref[:, :D_MODEL]
        w_lo = xs_ref[:, D_MODEL + 1:D_MODEL + 2]
        w_hi = xs_ref[:, D_MODEL + 2:D_MODEL + 3]
        h = _rms(x, g_ref[...]).astype(BF16)
        hid_lo = (_silu(_dot(h, wg_lo_ref[...])) * _dot(h, wu_lo_ref[...]) * w_lo).astype(BF16)
        hid_hi = (_silu(_dot(h, wg_hi_ref[...])) * _dot(h, wu_hi_ref[...]) * w_hi).astype(BF16)
        o_ref[...] = x + (_dot(hid_lo, wd_lo_ref[...]) + _dot(hid_hi, wd_hi_ref[...]))

    @pl.when(valid_ref[j] == 0)
    def _():
        o_ref[...] = jnp.zeros(o_ref.shape, F32)


def _moe(xs, g_ffn, wg, wu, wd, in_blk, e_lo, e_hi, valid):
    tm = TOKEN_TILE
    n_tiles = xs.shape[0] // tm
    lo = lambda j, blk, elo, ehi, va: (elo[j], 0, 0)
    hi = lambda j, blk, elo, ehi, va: (ehi[j], 0, 0)
    up = pl.BlockSpec((None, D_MODEL, D_EXPERT), lo), pl.BlockSpec((None, D_MODEL, D_EXPERT), hi)
    down = pl.BlockSpec((None, D_EXPERT, D_MODEL), lo), pl.BlockSpec((None, D_EXPERT, D_MODEL), hi)
    grid_spec = pltpu.PrefetchScalarGridSpec(
        num_scalar_prefetch=4,
        grid=(n_tiles,),
        in_specs=[pl.BlockSpec((tm, ROW_W), lambda j, blk, elo, ehi, va: (blk[j], 0)),
                  pl.BlockSpec((1, D_MODEL), lambda j, blk, elo, ehi, va: (0, 0)),
                  up[0], up[1], up[0], up[1], down[0], down[1]],
        out_specs=pl.BlockSpec((tm, D_MODEL), lambda j, blk, elo, ehi, va: (j, 0)),
    )
    return pl.pallas_call(
        _moe_kernel,
        grid_spec=grid_spec,
        out_shape=jax.ShapeDtypeStruct((xs.shape[0], D_MODEL), F32),
        compiler_params=_cparams(1),
        name="moe",
    )(in_blk, e_lo, e_hi, valid, xs, g_ffn, wg, wg, wu, wu, wd, wd)


def _ple_kernel(pos_ref, pos_next_ref, xs_hbm, p_ref, gple_ref, wpg_ref, wp_ref, gfin_ref, y_ref, buf, sem):
    i = pl.program_id(0)
    tm = buf.shape[1]
    slot = i % 2

    def gather(idx_ref, s):
        for r in range(tm):
            pltpu.make_async_copy(xs_hbm.at[pl.ds(idx_ref[0, r], 1)], buf.at[s, pl.ds(r, 1)],
                                  sem.at[s]).start(priority=r % 2)

    @pl.when(i == 0)
    def _():
        gather(pos_ref, 0)

    @pl.when(i + 1 < pl.num_programs(0))
    def _():
        gather(pos_next_ref, 1 - slot)

    pltpu.make_async_copy(xs_hbm.at[pl.ds(0, tm)], buf.at[slot], sem.at[slot]).wait()

    x2 = buf[slot]
    hp = _rms(x2, gple_ref[...]).astype(BF16)
    gate = jax.nn.sigmoid(_dot(hp, wpg_ref[...]))
    x3 = x2 + gate * _dot(p_ref[...].astype(BF16), wp_ref[...])
    y_ref[...] = _rms(x3, gfin_ref[...])


def _ple(xs2, pos2d, p, g_ple, wpg, wp, g_final):
    n = p.shape[0]
    tm = TOKEN_TILE
    row = lambda i: (i, 0)
    fixed = lambda i: (0, 0)
    last = n // tm - 1
    return pl.pallas_call(
        _ple_kernel,
        grid=(n // tm,),
        in_specs=[pl.BlockSpec((None, 1, tm), lambda i: (i, 0, 0), memory_space=pltpu.SMEM),
                  pl.BlockSpec((None, 1, tm), lambda i: (jnp.minimum(i + 1, last), 0, 0), memory_space=pltpu.SMEM),
                  pl.BlockSpec(memory_space=pl.ANY),
                  pl.BlockSpec((tm, P_DIM), row), pl.BlockSpec((1, D_MODEL), fixed),
                  pl.BlockSpec((D_MODEL, D_MODEL), fixed), pl.BlockSpec((P_DIM, D_MODEL), fixed),
                  pl.BlockSpec((1, D_MODEL), fixed)],
        out_specs=pl.BlockSpec((tm, D_MODEL), row),
        out_shape=jax.ShapeDtypeStruct((n, D_MODEL), F32),
        scratch_shapes=[pltpu.VMEM((2, tm, D_MODEL), F32), pltpu.SemaphoreType.DMA((2,))],
        compiler_params=_cparams(1),
        name="ple",
    )(pos2d, pos2d, xs2, p, g_ple, wpg, wp, g_final)


def _positions_kernel(off_ref, route_ref, pos_ref):
    route = route_ref[...]
    tm = route.shape[0]
    lane = lax.broadcasted_iota(I32, (tm, LANES), 1)
    cls = route[:, 0:1].astype(I32)
    first = jnp.sum(jnp.where(lane == cls, off_ref[...], 0.0), axis=-1, keepdims=True)
    pos = first + route[:, 3:4]
    pos_ref[...] = jnp.where(lane == 0, pos, 0.0).T[0:1, :].astype(I32)


def _positions(xr, class_row_start, n):
    tm = TOKEN_TILE
    return pl.pallas_call(
        _positions_kernel,
        grid=(n // tm,),
        in_specs=[pl.BlockSpec((1, LANES), lambda i: (0, 0)),
                  pl.BlockSpec((tm, LANES), lambda i: (i, D_MODEL // LANES))],
        out_specs=pl.BlockSpec((None, 1, tm), lambda i: (i, 0, 0)),
        out_shape=jax.ShapeDtypeStruct((n // tm, 1, tm), I32),
        compiler_params=_cparams(1),
        name="positions",
    )(class_row_start, xr)


def _routing_plan(xr, counts, n):
    tm = TOKEN_TILE
    n_tiles = n // tm + N_CLASSES
    cnt = counts[0, :N_CLASSES].astype(I32)
    tiles = (cnt + tm - 1) // tm
    tile_end = jnp.cumsum(tiles)
    tile_start = tile_end - tiles
    n_used = tile_end[-1]
    class_row_start = jnp.pad((tile_start * tm).astype(F32), (0, LANES - N_CLASSES)).reshape(1, LANES)
    pos3d = _positions(xr, class_row_start, n)
    j = jnp.arange(n_tiles, dtype=I32)
    valid = j < n_used
    in_blk = jnp.minimum(j, n_used - 1)
    tcls = jnp.sum((tile_end[None, :] <= in_blk[:, None]).astype(I32), axis=1)
    grp = tcls // N_PAIRS
    pair = tcls % N_PAIRS
    e_lo = grp * EXPERTS_PER_GROUP + jnp.asarray(PAIR_LO, I32)[pair]
    e_hi = grp * EXPERTS_PER_GROUP + jnp.asarray(PAIR_HI, I32)[pair]
    seg_zero = jnp.where(tiles > 0, tile_end * tm - tm, -1)
    tail = n_used + jnp.arange(N_CLASSES, dtype=I32)
    tail_zero = jnp.where(tail < n_tiles, tail * tm, -1)
    zero_starts = jnp.concatenate([seg_zero, tail_zero]).astype(I32)
    return pos3d, zero_starts, in_blk, e_lo, e_hi, valid.astype(I32), n_tiles * tm


def _rope_tables(pos, rows):
    half = DH_A // 2
    inv_freq = ROPE_THETA ** (-jnp.arange(half, dtype=F32) / half)
    ang = pos.astype(F32)[:, None] * inv_freq[None, :]
    cos = jnp.cos(ang)
    sin = jnp.sin(ang)
    cos_t = jnp.tile(jnp.concatenate([cos, cos], axis=-1), (1, ATT_W // DH_A))
    sin_t = jnp.tile(jnp.concatenate([-sin, sin], axis=-1), (1, ATT_W // DH_A))
    reps = max(rows // pos.shape[0], 1)
    return jnp.tile(cos_t, (reps, 1)), jnp.tile(sin_t, (reps, 1))


def _layer(i, x, p, pos, attend, s0, w, batch, seq, transposed):
    n = batch * seq
    small = BF16 if seq % 16 == 0 else F32
    cos_t, sin_t = _rope_tables(pos, TOKEN_TILE)
    proj = _inproj(x, w["g_mix"], w["w_seq"], cos_t, sin_t, batch, seq, transposed)
    lam_init = 0.8 - 0.6 * math.exp(-0.3 * i)
    ya, kf, vf = attend(proj, lam_init)
    qr, fl, ir, gr = proj[-4:]
    yr, s_new = _hgrn(qr, fl, ir, gr, w["lb"], w["g_rec"], s0, batch, seq, small)
    xr, counts = _mix(x, ya.astype(BF16), yr.astype(BF16), w["g_mix"], w["w_gate_a"], w["w_gate_r"], w["w_branch_a"],
                      w["w_branch_r"], w["w_out"], w["g_ffn"], w["w_route"], w["b_route"])
    pos2d, zero_starts, in_blk, e_lo, e_hi, valid, n_sorted = _routing_plan(xr, counts, n)
    xs = _dispatch(xr, pos2d, zero_starts, n_sorted)
    xs2 = _moe(xs, w["g_ffn"], w["w_exp_gate"], w["w_exp_up"], w["w_exp_down"], in_blk, e_lo, e_hi, valid)
    y = _ple(xs2, pos2d, p, w["g_ple"], w["w_ple_gate"], w["w_ple"], w["g_final"])
    return y, kf, vf, s_new


def kernel(x_prompt, x_sample, p_prompt, p_sample, cache_k, cache_v, state_hgrn, page_table, g_mix, w_in, lam,
           g_subln, lb_param, g_rec, w_branch_a, w_branch_r, w_out, g_ffn, w_route_group, b_route_group,
           w_route_expert, b_route_expert, w_exp_gate, w_exp_up, w_exp_down, g_ple, w_ple_gate, w_ple, g_final):
    depth = w_in.shape[0]
    assert depth == 1, "single-layer step"
    bp, sp, _ = x_prompt.shape
    bs, ts, _ = x_sample.shape
    past_len = page_table.shape[1] * PAGE_SIZE
    i = 0

    w_in_b = w_in[i].astype(BF16)
    n_route = N_GROUPS + N_EXPERTS
    w_route = jnp.concatenate([w_route_group[i], w_route_expert[i]], axis=1)
    w_route = jnp.pad(w_route, ((0, 0), (0, LANES - n_route))).astype(BF16)
    b_route = jnp.pad(jnp.concatenate([b_route_group[i], b_route_expert[i]]), (0, LANES - n_route)).reshape(1, LANES)
    lb = jnp.cumsum(jax.nn.softmax(lb_param.astype(F32), axis=0), axis=0)[i].reshape(1, REC_W)
    w = dict(
        g_mix=g_mix[i].reshape(1, D_MODEL), w_seq=w_in_b[:, :N_SEQ_COLS],
        w_gate_a=w_in_b[:, N_SEQ_COLS:N_SEQ_COLS + D_MODEL], w_gate_r=w_in_b[:, N_SEQ_COLS + D_MODEL:],
        lb=lb, g_rec=g_rec[i].reshape(1, V_R),
        w_branch_a=w_branch_a[i].astype(BF16), w_branch_r=w_branch_r[i].astype(BF16), w_out=w_out[i].astype(BF16),
        g_ffn=g_ffn[i].reshape(1, D_MODEL), w_route=w_route, b_route=b_route,
        w_exp_gate=w_exp_gate[i].astype(BF16), w_exp_up=w_exp_up[i].astype(BF16), w_exp_down=w_exp_down[i].astype(BF16),
        g_ple=g_ple[i].reshape(1, D_MODEL), w_ple_gate=w_ple_gate[i].astype(BF16), w_ple=w_ple[i].astype(BF16),
        g_final=g_final.reshape(1, D_MODEL),
    )
    lam_i = lam[i].astype(F32)
    g_sub = g_subln[i].reshape(1, E_A)

    def attend_prompt(proj, lam_init):
        q_t, k_t, kb, vf, v_t = proj[:5]
        return _attn_prompt(q_t, kb, v_t, lam_i, g_sub.reshape(E_A, 1), bp, sp, lam_init), k_t, vf

    n_phys = cache_k.shape[1]
    ckt = jnp.transpose(cache_k[i], (0, 2, 3, 4, 1)).reshape(n_phys, ATT_W, PAGE_SIZE)
    cv2 = cache_v[i].reshape(n_phys, PAGE_SIZE * H_A, E_A)

    def attend_sample(proj, lam_init):
        q, kf, vf = proj[:3]
        return _attn_sample(q, kf, vf, ckt, cv2, page_table, lam_i, g_sub, bs, ts, lam_init), kf, vf

    pos_p = jnp.arange(sp, dtype=I32)
    pos_s = past_len + jnp.arange(ts, dtype=I32)
    s0_p = jnp.zeros((bp, H_R, K_R, V_R), F32)

    y_p, k_t, v_p, s_p = _layer(i, x_prompt.reshape(bp * sp, D_MODEL), p_prompt[i].reshape(bp * sp, P_DIM),
                                pos_p, attend_prompt, s0_p, w, bp, sp, True)
    y_s, k_s, v_s, s_s = _layer(i, x_sample.reshape(bs * ts, D_MODEL), p_sample[i].reshape(bs * ts, P_DIM),
                                pos_s, attend_sample, state_hgrn[i], w, bs, ts, False)
    k_p = jnp.transpose(k_t.reshape(1, bp, H_A, 2, DH_A, sp), (0, 1, 5, 2, 3, 4))

    return (y_p.reshape(bp, sp, D_MODEL), y_s.reshape(bs, ts, D_MODEL),
            k_p, v_p.reshape(1, bp, sp, H_A, E_A), s_p.reshape(1, bp, H_R, K_R, V_R),
            k_s.reshape(1, bs, ts, H_A, 2, DH_A), v_s.reshape(1, bs, ts, H_A, E_A), s_s.reshape(1, bs, H_R, K_R, V_R))
```

```python
import functools
import math

import jax
import jax.numpy as jnp
import numpy as np
from jax import lax
from jax.experimental import pallas as pl
from jax.experimental.pallas import tpu as pltpu

F32 = jnp.float32
BF16 = jnp.bfloat16
I32 = jnp.int32

D_MODEL = 1024
P_DIM = 256
H_A = 4
DH_A = 64
E_A = 2 * DH_A
H_R = 4
K_R = 128
V_R = 128
HGRN_CHUNK = 32
N_GROUPS = 4
EXPERTS_PER_GROUP = 4
N_EXPERTS = N_GROUPS * EXPERTS_PER_GROUP
D_EXPERT = 256
ROPE_THETA = 10000.0
EPS = 1e-6
NEG_INF = -1e30
LOG2_E = 1.4426950408889634
PAGE_SIZE = 128
ATT_W = H_A * E_A
REC_W = H_R * V_R
N_SEQ_COLS = 7 * 512
LANES = 128
N_PAIRS = 6
N_CLASSES = N_GROUPS * N_PAIRS
PAIR_LO = (0, 0, 0, 1, 1, 2)
PAIR_HI = (1, 2, 3, 2, 3, 3)
ROW_W = D_MODEL + LANES
TOKEN_TILE = 256
POSITION_TILES = 8
MIX_TILE = 512
MIX_PART = 256
ATTN_TILE = 512
QUERY_GROUP = 256
SCORE_LOOKAHEAD = 4
PAGES_PER_STEP = 16
VMEM_LIMIT = 56 * 1024 * 1024


def _cparams(n_axes, flags=None):
    return pltpu.CompilerParams(dimension_semantics=("arbitrary",) * n_axes, vmem_limit_bytes=VMEM_LIMIT, flags=flags)


def _rms(x, g):
    return x * lax.rsqrt(jnp.mean(x * x, axis=-1, keepdims=True) + EPS) * g


def _silu(x):
    return x * jax.nn.sigmoid(x)


def _dot(a, b):
    return jnp.dot(a, b, preferred_element_type=F32)


def _dot_nt(a, b):
    return lax.dot_general(a, b, (((1,), (1,)), ((), ())), preferred_element_type=F32)


def _inproj_kernel(x_ref, g_ref, w_ref, cos_ref, sin_ref, *out_refs, transposed):
    h = _rms(x_ref[...], g_ref[...]).astype(BF16)
    cos = cos_ref[...]
    sin = sin_ref[...]
    tm = h.shape[0]
    lane = lax.broadcasted_iota(I32, (tm, ATT_W), 1)
    first_half = (lane % DH_A) < (DH_A // 2)

    def proj(c):
        return _dot(h, w_ref[:, c * 512:(c + 1) * 512])

    def rope(t):
        rot = jnp.where(first_half, pltpu.roll(t, ATT_W - DH_A // 2, 1), pltpu.roll(t, DH_A // 2, 1))
        return t * cos + rot * sin

    q = rope(proj(0)) * (DH_A ** -0.5)
    k = rope(proj(1))
    v = proj(2)
    if transposed:
        qt_ref, kt_ref, kb_ref, vf_ref, vt_ref = out_refs[:5]
        qt_ref[...] = (q * LOG2_E).T.astype(BF16)
        kt_ref[...] = k.T
        kb_ref[...] = k.astype(BF16)
        for head in range(H_A):
            vf_ref[pl.ds(head, tm, stride=H_A), :] = v[:, head * E_A:(head + 1) * E_A]
        vt_ref[...] = v.T.astype(BF16)
        rest = out_refs[5:]
    else:
        q_ref, kf_ref, vf_ref = out_refs[:3]
        q_ref[...] = q
        kf_ref[...] = k
        vf_ref[...] = v
        rest = out_refs[3:]
    qr_ref, fl_ref, ir_ref, gr_ref = rest
    qr_ref[...] = _silu(proj(3))
    fl_ref[...] = proj(4)
    ir_ref[...] = proj(5)
    gr_ref[...] = proj(6)


def _inproj(x, g_mix, w_seq, cos_t, sin_t, batch, seq, transposed):
    n = x.shape[0]
    tm = TOKEN_TILE
    nblk = max(seq // tm, 1)
    row = lambda i: (i, 0)
    tab = lambda i: (i % nblk, 0)
    const = lambda i: (0, 0)
    o512 = pl.BlockSpec((tm, 512), row)
    sds = lambda dt: jax.ShapeDtypeStruct((n, 512), dt)
    if transposed:
        ot = pl.BlockSpec((None, 512, tm), lambda i: (i // nblk, 0, i % nblk))
        sdt = lambda dt: jax.ShapeDtypeStruct((batch, 512, seq), dt)
        ov = pl.BlockSpec((tm * H_A, E_A), row)
        out_specs = [ot, ot, o512, ov, ot] + [o512] * 4
        out_shape = ([sdt(BF16), sdt(F32), sds(BF16), jax.ShapeDtypeStruct((n * H_A, E_A), F32), sdt(BF16)]
                     + [sds(F32)] * 4)
    else:
        out_specs = [o512] * 7
        out_shape = [sds(F32)] * 7
    return pl.pallas_call(
        functools.partial(_inproj_kernel, transposed=transposed),
        grid=(n // tm,),
        in_specs=[pl.BlockSpec((tm, D_MODEL), row), pl.BlockSpec((1, D_MODEL), const),
                  pl.BlockSpec((D_MODEL, N_SEQ_COLS), const),
                  pl.BlockSpec((tm, ATT_W), tab), pl.BlockSpec((tm, ATT_W), tab)],
        out_specs=out_specs,
        out_shape=out_shape,
        compiler_params=_cparams(1),
        name="inproj",
    )(x, g_mix, w_seq, cos_t, sin_t)


def _lambda_value(lam_ref):
    lam = lam_ref[...]
    s01 = jnp.sum(lam[0:1, :] * lam[1:2, :], axis=-1, keepdims=True)
    s23 = jnp.sum(lam[2:3, :] * lam[3:4, :], axis=-1, keepdims=True)
    return jnp.exp(s01) - jnp.exp(s23)


def _diff_norm(o0, o1, lam_val, g_sub, lam_init):
    od = o0 - lam_val * o1
    return _rms(od, g_sub) * (1.0 - lam_init)


def _split_maps(qh):
    lane = lax.broadcasted_iota(I32, qh.shape, 1)
    zero = jnp.zeros_like(qh)
    return jnp.concatenate([jnp.where(lane < DH_A, qh, zero), jnp.where(lane >= DH_A, qh, zero)], axis=0)


def _attn_prompt_kernel(q_ref, k_ref, v_ref, lam_ref, g_ref, o_ref, q2_scr, *state, lam_init):
    i = pl.program_id(1)
    tq = q_ref.shape[1]
    chains = [(h, c0) for h in range(H_A) for c0 in range(0, 2 * tq, QUERY_GROUP)]
    m_scrs, l_scrs, acc_scrs = state[0::3], state[1::3], state[2::3]
    feat = lax.broadcasted_iota(I32, (E_A, tq), 0)
    for h in range(H_A):
        qh = q_ref[h * E_A:(h + 1) * E_A, :].astype(F32)
        q2_scr[h] = jnp.concatenate([jnp.where(feat < DH_A, qh, 0.0), jnp.where(feat >= DH_A, qh, 0.0)],
                                    axis=1).astype(BF16)
    for n in range(len(chains)):
        m_scrs[n][...] = jnp.full(m_scrs[n].shape, NEG_INF, F32)
        l_scrs[n][...] = jnp.zeros(l_scrs[n].shape, F32)
        acc_scrs[n][...] = jnp.zeros(acc_scrs[n].shape, F32)

    def key_block(kb, on_diagonal):
        r0 = pl.multiple_of(kb * tq, tq)

        def n_keys(c0):
            return min(tq, c0 % tq + QUERY_GROUP) if on_diagonal else tq

        def scores(n):
            h, c0 = chains[n]
            nk = n_keys(c0)
            s = _dot(k_ref[pl.ds(r0, nk), h * E_A:(h + 1) * E_A], q2_scr[h, :, c0:c0 + QUERY_GROUP])
            if on_diagonal:
                key = lax.broadcasted_iota(I32, (nk, QUERY_GROUP), 0)
                qry = (lax.broadcasted_iota(I32, (nk, QUERY_GROUP), 1) + c0) % tq
                s = jnp.where(key > qry, NEG_INF, s)
            return s

        ahead = [scores(n) for n in range(min(SCORE_LOOKAHEAD, len(chains)))]
        for n, (h, c0) in enumerate(chains):
            cs = slice(h * E_A, (h + 1) * E_A)
            s = ahead.pop(0)
            if n + SCORE_LOOKAHEAD < len(chains):
                ahead.append(scores(n + SCORE_LOOKAHEAD))
            m_old = m_scrs[n][...]
            m_new = jnp.maximum(m_old, jnp.max(s, axis=0, keepdims=True))
            alpha = jnp.exp2(m_old - m_new)
            p = jnp.exp2(s - m_new)
            l_scrs[n][...] = l_scrs[n][...] * alpha + jnp.sum(p, axis=0, keepdims=True)
            acc_scrs[n][...] = acc_scrs[n][...] * alpha + _dot(v_ref[cs, pl.ds(r0, n_keys(c0))], p.astype(BF16))
            m_scrs[n][...] = m_new

    def below_diagonal(kb, carry):
        key_block(kb, False)
        return carry

    lax.fori_loop(0, i, below_diagonal, 0)
    key_block(i, True)

    lam_val = _lambda_value(lam_ref) + lam_init
    per_head = len(chains) // H_A
    for h in range(H_A):
        o = jnp.concatenate([acc_scrs[n][...] / l_scrs[n][...] for n in range(h * per_head, (h + 1) * per_head)],
                            axis=1)
        od = o[:, :tq] - lam_val * o[:, tq:]
        ms = jnp.mean(od * od, axis=0, keepdims=True)
        y_t = od * lax.rsqrt(ms + EPS) * g_ref[...] * (1.0 - lam_init)
        o_ref[:, h * E_A:(h + 1) * E_A] = y_t.T.astype(o_ref.dtype)


def _attn_prompt(q_t, k, v_t, lam, g_sub_col, batch, seq, lam_init):
    tq = min(ATTN_TILE, seq)
    nq = seq // tq
    n_chains = H_A * (2 * tq // QUERY_GROUP)
    state = [pltpu.VMEM((1, QUERY_GROUP), F32), pltpu.VMEM((1, QUERY_GROUP), F32),
             pltpu.VMEM((E_A, QUERY_GROUP), F32)] * n_chains
    return pl.pallas_call(
        functools.partial(_attn_prompt_kernel, lam_init=lam_init),
        grid=(batch, nq),
        in_specs=[pl.BlockSpec((None, ATT_W, tq), lambda b, i: (b, 0, i)),
                  pl.BlockSpec((seq, ATT_W), lambda b, i: (b, 0)),
                  pl.BlockSpec((None, ATT_W, seq), lambda b, i: (b, 0, 0)),
                  pl.BlockSpec((4, DH_A), lambda b, i: (0, 0)),
                  pl.BlockSpec((E_A, 1), lambda b, i: (0, 0))],
        out_specs=pl.BlockSpec((tq, ATT_W), lambda b, i: (b * nq + i, 0)),
        out_shape=jax.ShapeDtypeStruct(k.shape, BF16),
        scratch_shapes=[pltpu.VMEM((H_A, E_A, 2 * tq), BF16)] + state,
        compiler_params=_cparams(2),
        name="attn_prompt",
    )(q_t, k, v_t, lam, g_sub_col)


def _attn_sample_kernel(pt_ref, q_ref, kn_ref, vn_ref, lam_ref, g_ref, *refs, lam_init, pages):
    k_refs = refs[:pages]
    v_refs = refs[pages:2 * pages]
    o_ref = refs[2 * pages]
    m_scr, l_scr, acc_scr = refs[2 * pages + 1:]
    j = pl.program_id(1)
    t = q_ref.shape[0]

    @pl.when(j == 0)
    def _():
        m_scr[...] = jnp.full(m_scr.shape, NEG_INF, F32)
        l_scr[...] = jnp.zeros(l_scr.shape, F32)
        acc_scr[...] = jnp.zeros(acc_scr.shape, F32)

    rows_h = 2 * t

    def own_head_blocks(big):
        return jnp.concatenate([big[h * rows_h:(h + 1) * rows_h, h * E_A:(h + 1) * E_A] for h in range(H_A)], axis=0)

    def update(s, v_all):
        m_old = m_scr[...]
        m_new = jnp.maximum(m_old, jnp.max(s, axis=-1, keepdims=True))
        alpha = jnp.exp(m_old - m_new)
        p = jnp.exp(s - m_new)
        l_scr[...] = l_scr[...] * alpha + jnp.sum(p, axis=-1, keepdims=True)
        acc_scr[...] = acc_scr[...] * alpha + own_head_blocks(_dot(p.astype(BF16), v_all))
        m_scr[...] = m_new

    q = q_ref[...]
    zero = jnp.zeros((rows_h, E_A), F32)
    qall = jnp.concatenate(
        [jnp.concatenate([_split_maps(q[:, h * E_A:(h + 1) * E_A]) if hh == h else zero for hh in range(H_A)], axis=1)
         for h in range(H_A)], axis=0).astype(BF16)

    kt = jnp.concatenate([k_refs[r][...].astype(BF16) for r in range(pages)], axis=1)
    v_all = jnp.concatenate(
        [jnp.concatenate([v_refs[r][pl.ds(h, PAGE_SIZE, stride=H_A), :] for h in range(H_A)], axis=1)
         for r in range(pages)], axis=0).astype(BF16)
    update(_dot(qall, kt), v_all)

    @pl.when(j == pl.num_programs(1) - 1)
    def _():
        lam_val = _lambda_value(lam_ref) + lam_init
        n_rows = H_A * rows_h
        row_t = lax.broadcasted_iota(I32, (n_rows, PAGE_SIZE), 0) % t
        col = lax.broadcasted_iota(I32, (n_rows, PAGE_SIZE), 1)
        visible = col <= row_t
        pad = jnp.zeros((PAGE_SIZE - t, ATT_W), F32)
        kn_t = jnp.concatenate([kn_ref[...], pad], axis=0).T.astype(BF16)
        vn = jnp.concatenate([vn_ref[...], pad], axis=0).astype(BF16)
        update(jnp.where(visible, _dot(qall, kn_t), NEG_INF), vn)
        o = acc_scr[...] / l_scr[...]
        for h in range(H_A):
            o0 = o[h * rows_h:h * rows_h + t]
            o1 = o[h * rows_h + t:(h + 1) * rows_h]
            y = _diff_norm(o0, o1, lam_val, g_ref[...], lam_init)
            o_ref[:, h * E_A:(h + 1) * E_A] = y.astype(BF16).astype(o_ref.dtype)


def _attn_sample(q, kn, vn, cache_kt, cache_v2, page_table, lam, g_sub, batch, t, lam_init):
    pages = PAGES_PER_STEP
    n_pages = page_table.shape[1]
    steps = n_pages // pages
    fixed = lambda b, j, pt: (0, 0)
    rowb = lambda b, j, pt: (b, 0)
    n_rows = H_A * 2 * t

    def page_spec(r):
        return pl.BlockSpec((None, ATT_W, PAGE_SIZE), lambda b, j, pt: (pt[b, j * pages + r], 0, 0))

    grid_spec = pltpu.PrefetchScalarGridSpec(
        num_scalar_prefetch=1,
        grid=(batch, steps),
        in_specs=[pl.BlockSpec((t, ATT_W), rowb), pl.BlockSpec((t, ATT_W), rowb), pl.BlockSpec((t, ATT_W), rowb),
                  pl.BlockSpec((4, DH_A), fixed), pl.BlockSpec((1, E_A), fixed)]
                 + [page_spec(r) for r in range(pages)] + [page_spec(r) for r in range(pages)],
        out_specs=pl.BlockSpec((t, ATT_W), rowb),
        scratch_shapes=[pltpu.VMEM((n_rows, 1), F32), pltpu.VMEM((n_rows, 1), F32), pltpu.VMEM((n_rows, E_A), F32)],
    )
    return pl.pallas_call(
        functools.partial(_attn_sample_kernel, lam_init=lam_init, pages=pages),
        grid_spec=grid_spec,
        out_shape=jax.ShapeDtypeStruct(q.shape, F32),
        compiler_params=_cparams(2),
        name="attn_sample",
    )(page_table, q, kn, vn, lam, g_sub, *([cache_kt] * pages), *([cache_v2] * pages))


def _split3(x):
    a = x.astype(BF16)
    r = x - a.astype(F32)
    b = r.astype(BF16)
    c = (r - b.astype(F32)).astype(BF16)
    return a, b, c


def _hgrn_kernel(qr_ref, fl_ref, ir_ref, gr_ref, lb_ref, g_ref, s0_ref, y_ref, s_ref,
                 qe_scr, kk_scr, dl_scr, o_scr, st_scr, *, chunk, blk):
    t = qr_ref.shape[0]
    lb = lb_ref[...]
    row = lax.broadcasted_iota(I32, (blk, blk), 0)
    col = lax.broadcasted_iota(I32, (blk, blk), 1)
    same = (row // chunk) == (col // chunk)
    causal = jnp.logical_and(same, col <= row)
    cum_mask = jnp.where(causal, 1.0, 0.0).astype(BF16)
    all_mask = jnp.where(same, 1.0, 0.0).astype(BF16)

    def gates(b, carry):
        rs = pl.ds(pl.multiple_of(b * blk, blk), blk)
        f = lb + (1.0 - lb) * jax.nn.sigmoid(fl_ref[rs, :])
        g = jnp.log(f)
        k = 1.0 - f
        g1, g2, g3 = _split3(g)
        bcum = _dot(cum_mask, g1) + _dot(cum_mask, g2) + _dot(cum_mask, g3)
        blast = _dot(all_mask, g1) + _dot(all_mask, g2) + _dot(all_mask, g3)
        qe = (qr_ref[rs, :] * jnp.exp(bcum)).astype(BF16)
        kd = (k * jnp.exp(-bcum)).astype(BF16)
        qe_scr[rs, :] = qe.astype(qe_scr.dtype)
        kk_scr[rs, :] = (k * jnp.exp(blast - bcum)).astype(kk_scr.dtype)
        dl_scr[rs, :] = jnp.exp(blast)
        v = ir_ref[rs, :].astype(BF16)
        heads = [slice(h * K_R, (h + 1) * K_R) for h in range(H_R)]
        a = [jnp.where(causal, _dot_nt(qe[:, cs], kd[:, cs]), 0.0).astype(BF16) for cs in heads]
        for cs, a_h in zip(heads, a):
            o_scr[rs, cs] = _dot(a_h, v[:, cs])
        return carry

    lax.fori_loop(0, t // blk, gates, 0, unroll=2 if t // blk >= 2 else 1)

    for h in range(H_R):
        st_scr[h] = s0_ref[h].T

    n_chunks = t // chunk
    per_step = 2 if n_chunks % 2 == 0 else 1
    heads = [slice(h * K_R, (h + 1) * K_R) for h in range(H_R)]

    def step(j, carry):
        starts = [pl.multiple_of((j * per_step + u) * chunk, chunk) for u in range(per_step)]
        incs = [[_dot(ir_ref[pl.ds(r0, chunk), cs].T.astype(BF16), kk_scr[pl.ds(r0, chunk), cs].astype(BF16))
                 for cs in heads] for r0 in starts]
        for r0, inc in zip(starts, incs):
            rows = pl.ds(r0, chunk)
            for h, cs in enumerate(heads):
                st = st_scr[h]
                o_scr[rows, cs] = o_scr[rows, cs] + _dot_nt(qe_scr[rows, cs].astype(BF16), st.astype(BF16))
                st_scr[h] = st * dl_scr[pl.ds(r0, 1), cs] + inc[h]
        return carry

    lax.fori_loop(0, n_chunks // per_step, step, 0)
    for h in range(H_R):
        s_ref[h] = st_scr[h].T

    g_rec = g_ref[...]

    def finish(b, carry):
        rs = pl.ds(pl.multiple_of(b * blk, blk), blk)
        for h in range(H_R):
            cs = slice(h * K_R, (h + 1) * K_R)
            y = _rms(o_scr[rs, cs], g_rec) * _silu(gr_ref[rs, cs])
            y_ref[rs, cs] = y.astype(BF16).astype(y_ref.dtype)
        return carry

    lax.fori_loop(0, t // blk, finish, 0)


def _hgrn(qr, fl, ir, gr, lb, g_rec, s0, batch, t, out_dtype):
    chunk = math.gcd(t, HGRN_CHUNK)
    blk = min(t, 256)
    rowb = lambda b: (b, 0)
    fixed = lambda b: (0, 0)
    seq = pl.BlockSpec((t, REC_W), rowb)
    state = pl.BlockSpec((None, H_R, K_R, V_R), lambda b: (b, 0, 0, 0))
    return pl.pallas_call(
        functools.partial(_hgrn_kernel, chunk=chunk, blk=blk),
        grid=(batch,),
        in_specs=[seq, seq, seq, seq, pl.BlockSpec((1, REC_W), fixed), pl.BlockSpec((1, V_R), fixed), state],
        out_specs=[seq, state],
        out_shape=[jax.ShapeDtypeStruct(qr.shape, out_dtype), jax.ShapeDtypeStruct(s0.shape, F32)],
        scratch_shapes=[pltpu.VMEM((t, REC_W), out_dtype), pltpu.VMEM((t, REC_W), out_dtype),
                        pltpu.VMEM((t, REC_W), F32), pltpu.VMEM((t, REC_W), F32), pltpu.VMEM((H_R, V_R, K_R), F32)],
        compiler_params=_cparams(1),
        name="hgrn",
    )(qr, fl, ir, gr, lb, g_rec, s0)


def _mix_kernel(x_ref, ya_ref, yr_ref, gmix_ref, wga_ref, wgr_ref, wba_ref, wbr_ref, wout_ref,
                gffn_ref, wrt_ref, brt_ref, xr_ref, cnt_ref, cnt_scr):
    i = pl.program_id(0)
    tm = x_ref.shape[0]

    @pl.when(i == 0)
    def _():
        cnt_scr[...] = jnp.zeros(cnt_scr.shape, F32)

    part = min(tm, MIX_PART)
    parts = [slice(r, r + part) for r in range(0, tm, part)]
    xs = [x_ref[rs, :] for rs in parts]
    hs = [_rms(x, gmix_ref[...]).astype(BF16) for x in xs]
    gate_a = [jax.nn.sigmoid(_dot(h, wga_ref[...])) for h in hs]
    br_a = [_dot(ya_ref[rs, :], wba_ref[...]) for rs in parts]
    gate_r = [jax.nn.sigmoid(_dot(h, wgr_ref[...])) for h in hs]
    br_r = [_dot(yr_ref[rs, :], wbr_ref[...]) for rs in parts]
    merged = [(ga * ba + gr * br).astype(BF16) for ga, ba, gr, br in zip(gate_a, br_a, gate_r, br_r)]
    x1s = [x + _dot(m, wout_ref[...]) for x, m in zip(xs, merged)]
    for rs, x1 in zip(parts, x1s):
        xr_ref[rs, :D_MODEL] = x1
    hfs = [_rms(x1, gffn_ref[...]).astype(BF16) for x1 in x1s]
    lgs = [_dot(hf, wrt_ref[...]) + brt_ref[...] for hf in hfs]
    for rs, lg in zip(parts, lgs):
        xr_ref[rs, D_MODEL:] = _route_rows(lg, cnt_scr)
    cnt_ref[...] = cnt_scr[...]


def _route_rows(lg, cnt_scr):
    tm = lg.shape[0]
    lane = lax.broadcasted_iota(I32, (tm, LANES), 1)
    is_group = lane < N_GROUPS
    mg = jnp.max(jnp.where(is_group, lg, NEG_INF), axis=-1, keepdims=True)
    gidx = jnp.min(jnp.where(jnp.logical_and(is_group, lg == mg), lane, LANES), axis=-1, keepdims=True)
    p_top = 1.0 / jnp.sum(jnp.where(is_group, jnp.exp(lg - mg), 0.0), axis=-1, keepdims=True)
    base = N_GROUPS + EXPERTS_PER_GROUP * gidx
    in_grp = jnp.logical_and(lane >= base, lane < base + EXPERTS_PER_GROUP)
    v1 = jnp.max(jnp.where(in_grp, lg, NEG_INF), axis=-1, keepdims=True)
    e1 = jnp.min(jnp.where(jnp.logical_and(in_grp, lg == v1), lane, LANES), axis=-1, keepdims=True)
    rest = jnp.logical_and(in_grp, lane != e1)
    v2 = jnp.max(jnp.where(rest, lg, NEG_INF), axis=-1, keepdims=True)
    e2 = jnp.min(jnp.where(jnp.logical_and(rest, lg == v2), lane, LANES), axis=-1, keepdims=True)
    tt = jnp.exp(v2 - v1)
    w_a = (1.0 / (1.0 + tt)) * p_top
    w_b = (tt / (1.0 + tt)) * p_top
    a = e1 - base
    b = e2 - base
    a_first = a < b
    lo = jnp.minimum(a, b)
    hi = jnp.maximum(a, b)
    w_lo = jnp.where(a_first, w_a, w_b)
    w_hi = jnp.where(a_first, w_b, w_a)
    pair = jnp.where(lo == 0, hi - 1, jnp.where(lo == 1, hi + 1, 5))
    cls = gidx * N_PAIRS + pair

    onehot = lane == cls
    r_i = lax.broadcasted_iota(I32, (tm, tm), 0)
    c_i = lax.broadcasted_iota(I32, (tm, tm), 1)
    before = jnp.where(c_i < r_i, 1.0, 0.0).astype(BF16)
    excl = _dot(before, jnp.where(onehot, 1.0, 0.0).astype(BF16))
    rank = jnp.sum(jnp.where(onehot, excl + cnt_scr[...], 0.0), axis=-1, keepdims=True)
    cnt_scr[...] = cnt_scr[...] + jnp.sum(jnp.where(onehot, 1.0, 0.0), axis=0, keepdims=True)
    return jnp.where(lane == 0, cls.astype(F32),
                     jnp.where(lane == 1, w_lo, jnp.where(lane == 2, w_hi, jnp.where(lane == 3, rank, 0.0))))


def _mix(x, ya, yr, g_mix, wga, wgr, wba, wbr, wout, g_ffn, wrt, brt):
    n = x.shape[0]
    tm = min(MIX_TILE, n)
    row = lambda i: (i, 0)
    fixed = lambda i: (0, 0)
    full = lambda a: pl.BlockSpec(a.shape, fixed)
    return pl.pallas_call(
        _mix_kernel,
        grid=(n // tm,),
        in_specs=[pl.BlockSpec((tm, D_MODEL), row), pl.BlockSpec((tm, ATT_W), row), pl.BlockSpec((tm, REC_W), row),
                  full(g_mix), full(wga), full(wgr), full(wba), full(wbr), full(wout), full(g_ffn), full(wrt), full(brt)],
        out_specs=[pl.BlockSpec((tm, ROW_W), row), pl.BlockSpec((1, LANES), fixed)],
        out_shape=[jax.ShapeDtypeStruct((n, ROW_W), F32), jax.ShapeDtypeStruct((1, LANES), F32)],
        scratch_shapes=[pltpu.VMEM((1, LANES), F32)],
        compiler_params=_cparams(1),
        name="mix",
    )(x, ya, yr, g_mix, wga, wgr, wba, wbr, wout, g_ffn, wrt, brt)


def _row_copy_out(x_ref, o_hbm, sem, r, p):
    return pltpu.make_async_copy(x_ref.at[pl.ds(r, 1)], o_hbm.at[pl.ds(p, 1)], sem)


def _dispatch_kernel(zs_ref, pos_ref, x_ref, o_hbm, zero_scr, zsem, rsem):
    i = pl.program_id(0)
    tm = x_ref.shape[0]

    def zero_copy(c):
        start = pl.multiple_of(jnp.maximum(zs_ref[c], 0), tm)
        return pltpu.make_async_copy(zero_scr, o_hbm.at[pl.ds(start, tm)], zsem)

    @pl.when(i == 0)
    def _():
        zero_scr[...] = jnp.zeros(zero_scr.shape, F32)
        for c in range(zs_ref.shape[0]):
            @pl.when(zs_ref[c] >= 0)
            def _():
                zero_copy(c).start()
        for c in range(zs_ref.shape[0]):
            @pl.when(zs_ref[c] >= 0)
            def _():
                zero_copy(c).wait()

    for r in range(tm):
        _row_copy_out(x_ref, o_hbm, rsem, r, pos_ref[0, r]).start(priority=r % 2)
    pltpu.make_async_copy(x_ref, o_hbm.at[pl.ds(0, tm)], rsem).wait()


def _dispatch(xr, pos2d, zero_starts, n_rows_sorted):
    n = xr.shape[0]
    tm = TOKEN_TILE
    grid_spec = pltpu.PrefetchScalarGridSpec(
        num_scalar_prefetch=1,
        grid=(n // tm,),
        in_specs=[pl.BlockSpec((None, 1, tm), lambda i, zs: (i, 0, 0), memory_space=pltpu.SMEM),
                  pl.BlockSpec((tm, ROW_W), lambda i, zs: (i, 0))],
        out_specs=pl.BlockSpec(memory_space=pl.ANY),
        scratch_shapes=[pltpu.VMEM((tm, ROW_W), F32), pltpu.SemaphoreType.DMA(()), pltpu.SemaphoreType.DMA(())],
    )
    return pl.pallas_call(
        _dispatch_kernel,
        grid_spec=grid_spec,
        out_shape=jax.ShapeDtypeStruct((n_rows_sorted, ROW_W), F32),
        compiler_params=_cparams(1),
        name="dispatch",
    )(zero_starts, pos2d, xr)


MOE_TILES_PER_STEP = 2


def _moe_kernel(blk_ref, elo_ref, ehi_ref, valid_ref, g_ref, *refs):
    j = pl.program_id(0)
    k = MOE_TILES_PER_STEP
    xs_refs = refs[:k]
    w_refs = [refs[k + 6 * u:k + 6 * (u + 1)] for u in range(k)]
    o_ref = refs[7 * k]
    tm = xs_refs[0].shape[0]
    xs = [r[:, :D_MODEL] for r in xs_refs]
    hs = [_rms(x, g_ref[...]).astype(BF16) for x in xs]
    a_lo = [_dot(h, w[0][...]) for h, w in zip(hs, w_refs)]
    u_lo = [_dot(h, w[2][...]) for h, w in zip(hs, w_refs)]
    a_hi = [_dot(h, w[1][...]) for h, w in zip(hs, w_refs)]
    u_hi = [_dot(h, w[3][...]) for h, w in zip(hs, w_refs)]
    hid_lo = [(_silu(a) * u * r[:, D_MODEL + 1:D_MODEL + 2]).astype(BF16) for a, u, r in zip(a_lo, u_lo, xs_refs)]
    hid_hi = [(_silu(a) * u * r[:, D_MODEL + 2:D_MODEL + 3]).astype(BF16) for a, u, r in zip(a_hi, u_hi, xs_refs)]
    ys = [_dot(hl, w[4][...]) + _dot(hh, w[5][...]) for hl, hh, w in zip(hid_lo, hid_hi, w_refs)]
    for u in range(k):
        o_ref[u * tm:(u + 1) * tm, :] = jnp.where(valid_ref[k * j + u] == 1, xs[u] + ys[u], 0.0)


def _moe(xs, g_ffn, wg, wu, wd, in_blk, e_lo, e_hi, valid):
    tm = TOKEN_TILE
    k = MOE_TILES_PER_STEP
    n_tiles = xs.shape[0] // tm

    def tile_specs(u):
        lo = lambda j, blk, elo, ehi, va: (elo[k * j + u], 0, 0)
        hi = lambda j, blk, elo, ehi, va: (ehi[k * j + u], 0, 0)
        up = pl.BlockSpec((None, D_MODEL, D_EXPERT), lo), pl.BlockSpec((None, D_MODEL, D_EXPERT), hi)
        down = pl.BlockSpec((None, D_EXPERT, D_MODEL), lo), pl.BlockSpec((None, D_EXPERT, D_MODEL), hi)
        return [up[0], up[1], up[0], up[1], down[0], down[1]]

    def rows_spec(u):
        return pl.BlockSpec((tm, ROW_W), lambda j, blk, elo, ehi, va: (blk[k * j + u], 0))

    grid_spec = pltpu.PrefetchScalarGridSpec(
        num_scalar_prefetch=4,
        grid=(n_tiles // k,),
        in_specs=[pl.BlockSpec((1, D_MODEL), lambda j, blk, elo, ehi, va: (0, 0))]
                 + [rows_spec(u) for u in range(k)] + [s for u in range(k) for s in tile_specs(u)],
        out_specs=pl.BlockSpec((k * tm, D_MODEL), lambda j, blk, elo, ehi, va: (j, 0)),
    )
    return pl.pallas_call(
        _moe_kernel,
        grid_spec=grid_spec,
        out_shape=jax.ShapeDtypeStruct((xs.shape[0], D_MODEL), F32),
        compiler_params=_cparams(1),
        name="moe",
    )(in_blk, e_lo, e_hi, valid, g_ffn, *([xs] * k), *([wg, wg, wu, wu, wd, wd] * k))


def _ple_kernel(pos_ref, pos_next_ref, xs_hbm, p_ref, gple_ref, wpg_ref, wp_ref, gfin_ref, y_ref, buf, sem):
    i = pl.program_id(0)
    tm = buf.shape[1]
    slot = i % 2

    def gather(idx_ref, s):
        for r in range(tm):
            pltpu.make_async_copy(xs_hbm.at[pl.ds(idx_ref[0, r], 1)], buf.at[s, pl.ds(r, 1)],
                                  sem.at[s]).start(priority=r % 2)

    @pl.when(i == 0)
    def _():
        gather(pos_ref, 0)

    @pl.when(i + 1 < pl.num_programs(0))
    def _():
        gather(pos_next_ref, 1 - slot)

    pltpu.make_async_copy(xs_hbm.at[pl.ds(0, tm)], buf.at[slot], sem.at[slot]).wait()

    x2 = buf[slot]
    hp = _rms(x2, gple_ref[...]).astype(BF16)
    gate = jax.nn.sigmoid(_dot(hp, wpg_ref[...]))
    x3 = x2 + gate * _dot(p_ref[...].astype(BF16), wp_ref[...])
    y_ref[...] = _rms(x3, gfin_ref[...])


def _ple(xs2, pos2d, p, g_ple, wpg, wp, g_final):
    n = p.shape[0]
    tm = TOKEN_TILE
    row = lambda i: (i, 0)
    fixed = lambda i: (0, 0)
    last = n // tm - 1
    return pl.pallas_call(
        _ple_kernel,
        grid=(n // tm,),
        in_specs=[pl.BlockSpec((None, 1, tm), lambda i: (i, 0, 0), memory_space=pltpu.SMEM),
                  pl.BlockSpec((None, 1, tm), lambda i: (jnp.minimum(i + 1, last), 0, 0), memory_space=pltpu.SMEM),
                  pl.BlockSpec(memory_space=pl.ANY),
                  pl.BlockSpec((tm, P_DIM), row), pl.BlockSpec((1, D_MODEL), fixed),
                  pl.BlockSpec((D_MODEL, D_MODEL), fixed), pl.BlockSpec((P_DIM, D_MODEL), fixed),
                  pl.BlockSpec((1, D_MODEL), fixed)],
        out_specs=pl.BlockSpec((tm, D_MODEL), row),
        out_shape=jax.ShapeDtypeStruct((n, D_MODEL), F32),
        scratch_shapes=[pltpu.VMEM((2, tm, D_MODEL), F32), pltpu.SemaphoreType.DMA((2,))],
        compiler_params=_cparams(1),
        name="ple",
    )(pos2d, pos2d, xs2, p, g_ple, wpg, wp, g_final)


def _positions_kernel(off_ref, route_ref, pos_ref):
    n_sub, _, tm = pos_ref.shape
    lane = lax.broadcasted_iota(I32, (tm, LANES), 1)
    for u in range(n_sub):
        route = route_ref[u * tm:(u + 1) * tm, :]
        cls = route[:, 0:1].astype(I32)
        first = jnp.sum(jnp.where(lane == cls, off_ref[...], 0.0), axis=-1, keepdims=True)
        pos = first + route[:, 3:4]
        pos_ref[u] = jnp.where(lane == 0, pos, 0.0).T[0:1, :].astype(I32)


def _positions(xr, class_row_start, n):
    tm = TOKEN_TILE
    n_sub = min(POSITION_TILES, n // tm)
    return pl.pallas_call(
        _positions_kernel,
        grid=(n // (tm * n_sub),),
        in_specs=[pl.BlockSpec((1, LANES), lambda i: (0, 0)),
                  pl.BlockSpec((tm * n_sub, LANES), lambda i: (i, D_MODEL // LANES))],
        out_specs=pl.BlockSpec((n_sub, 1, tm), lambda i: (i, 0, 0)),
        out_shape=jax.ShapeDtypeStruct((n // tm, 1, tm), I32),
        compiler_params=_cparams(1),
        name="positions",
    )(class_row_start, xr)


def _routing_plan(xr, counts, n):
    tm = TOKEN_TILE
    n_tiles = pl.cdiv(n // tm + N_CLASSES, MOE_TILES_PER_STEP) * MOE_TILES_PER_STEP
    cnt = counts[0, :N_CLASSES].astype(I32)
    tiles = (cnt + tm - 1) // tm
    tile_end = jnp.cumsum(tiles)
    tile_start = tile_end - tiles
    n_used = tile_end[-1]
    class_row_start = jnp.pad((tile_start * tm).astype(F32), (0, LANES - N_CLASSES)).reshape(1, LANES)
    pos3d = _positions(xr, class_row_start, n)
    j = jnp.arange(n_tiles, dtype=I32)
    valid = j < n_used
    in_blk = jnp.minimum(j, n_used - 1)
    tcls = jnp.sum((tile_end[None, :] <= in_blk[:, None]).astype(I32), axis=1)
    grp = tcls // N_PAIRS
    pair = tcls % N_PAIRS
    e_lo = grp * EXPERTS_PER_GROUP + jnp.asarray(PAIR_LO, I32)[pair]
    e_hi = grp * EXPERTS_PER_GROUP + jnp.asarray(PAIR_HI, I32)[pair]
    seg_zero = jnp.where(tiles > 0, tile_end * tm - tm, -1)
    tail = n_used + jnp.arange(n_tiles - n // tm, dtype=I32)
    tail_zero = jnp.where(tail < n_tiles, tail * tm, -1)
    zero_starts = jnp.concatenate([seg_zero, tail_zero]).astype(I32)
    return pos3d, zero_starts, in_blk, e_lo, e_hi, valid.astype(I32), n_tiles * tm


def _rope_tables(pos, rows):
    half = DH_A // 2
    inv_freq = ROPE_THETA ** (-jnp.arange(half, dtype=F32) / half)
    ang = pos.astype(F32)[:, None] * inv_freq[None, :]
    cos = jnp.cos(ang)
    sin = jnp.sin(ang)
    cos_t = jnp.tile(jnp.concatenate([cos, cos], axis=-1), (1, ATT_W // DH_A))
    sin_t = jnp.tile(jnp.concatenate([-sin, sin], axis=-1), (1, ATT_W // DH_A))
    reps = max(rows // pos.shape[0], 1)
    return jnp.tile(cos_t, (reps, 1)), jnp.tile(sin_t, (reps, 1))


def _layer(i, x, p, pos, attend, s0, w, batch, seq, transposed):
    n = batch * seq
    small = BF16 if seq % 16 == 0 else F32
    cos_t, sin_t = _rope_tables(pos, TOKEN_TILE)
    proj = _inproj(x, w["g_mix"], w["w_seq"], cos_t, sin_t, batch, seq, transposed)
    lam_init = 0.8 - 0.6 * math.exp(-0.3 * i)
    ya, kf, vf = attend(proj, lam_init)
    qr, fl, ir, gr = proj[-4:]
    yr, s_new = _hgrn(qr, fl, ir, gr, w["lb"], w["g_rec"], s0, batch, seq, small)
    xr, counts = _mix(x, ya.astype(BF16), yr.astype(BF16), w["g_mix"], w["w_gate_a"], w["w_gate_r"], w["w_branch_a"],
                      w["w_branch_r"], w["w_out"], w["g_ffn"], w["w_route"], w["b_route"])
    pos2d, zero_starts, in_blk, e_lo, e_hi, valid, n_sorted = _routing_plan(xr, counts, n)
    xs = _dispatch(xr, pos2d, zero_starts, n_sorted)
    xs2 = _moe(xs, w["g_ffn"], w["w_exp_gate"], w["w_exp_up"], w["w_exp_down"], in_blk, e_lo, e_hi, valid)
    y = _ple(xs2, pos2d, p, w["g_ple"], w["w_ple_gate"], w["w_ple"], w["g_final"])
    return y, kf, vf, s_new


def kernel(x_prompt, x_sample, p_prompt, p_sample, cache_k, cache_v, state_hgrn, page_table, g_mix, w_in, lam,
           g_subln, lb_param, g_rec, w_branch_a, w_branch_r, w_out, g_ffn, w_route_group, b_route_group,
           w_route_expert, b_route_expert, w_exp_gate, w_exp_up, w_exp_down, g_ple, w_ple_gate, w_ple, g_final):
    depth = w_in.shape[0]
    assert depth == 1, "single-layer step"
    bp, sp, _ = x_prompt.shape
    bs, ts, _ = x_sample.shape
    past_len = page_table.shape[1] * PAGE_SIZE
    i = 0

    w_in_b = w_in[i].astype(BF16)
    n_route = N_GROUPS + N_EXPERTS
    w_route = jnp.concatenate([w_route_group[i], w_route_expert[i]], axis=1)
    w_route = jnp.pad(w_route, ((0, 0), (0, LANES - n_route))).astype(BF16)
    b_route = jnp.pad(jnp.concatenate([b_route_group[i], b_route_expert[i]]), (0, LANES - n_route)).reshape(1, LANES)
    lb = jnp.cumsum(jax.nn.softmax(lb_param.astype(F32), axis=0), axis=0)[i].reshape(1, REC_W)
    w = dict(
        g_mix=g_mix[i].reshape(1, D_MODEL), w_seq=w_in_b[:, :N_SEQ_COLS],
        w_gate_a=w_in_b[:, N_SEQ_COLS:N_SEQ_COLS + D_MODEL], w_gate_r=w_in_b[:, N_SEQ_COLS + D_MODEL:],
        lb=lb, g_rec=g_rec[i].reshape(1, V_R),
        w_branch_a=w_branch_a[i].astype(BF16), w_branch_r=w_branch_r[i].astype(BF16), w_out=w_out[i].astype(BF16),
        g_ffn=g_ffn[i].reshape(1, D_MODEL), w_route=w_route, b_route=b_route,
        w_exp_gate=w_exp_gate[i].astype(BF16), w_exp_up=w_exp_up[i].astype(BF16), w_exp_down=w_exp_down[i].astype(BF16),
        g_ple=g_ple[i].reshape(1, D_MODEL), w_ple_gate=w_ple_gate[i].astype(BF16), w_ple=w_ple[i].astype(BF16),
        g_final=g_final.reshape(1, D_MODEL),
    )
    lam_i = lam[i].astype(F32)
    g_sub = g_subln[i].reshape(1, E_A)

    def attend_prompt(proj, lam_init):
        q_t, k_t, kb, vf, v_t = proj[:5]
        return _attn_prompt(q_t, kb, v_t, lam_i, g_sub.reshape(E_A, 1), bp, sp, lam_init), k_t, vf

    n_phys = cache_k.shape[1]
    ckt = jnp.transpose(cache_k[i], (0, 2, 3, 4, 1)).reshape(n_phys, ATT_W, PAGE_SIZE)
    cv2 = cache_v[i].reshape(n_phys, PAGE_SIZE * H_A, E_A)

    def attend_sample(proj, lam_init):
        q, kf, vf = proj[:3]
        return _attn_sample(q, kf, vf, ckt, cv2, page_table, lam_i, g_sub, bs, ts, lam_init), kf, vf

    pos_p = jnp.arange(sp, dtype=I32)
    pos_s = past_len + jnp.arange(ts, dtype=I32)
    s0_p = jnp.zeros((bp, H_R, K_R, V_R), F32)

    y_p, k_t, v_p, s_p = _layer(i, x_prompt.reshape(bp * sp, D_MODEL), p_prompt[i].reshape(bp * sp, P_DIM),
                                pos_p, attend_prompt, s0_p, w, bp, sp, True)
    y_s, k_s, v_s, s_s = _layer(i, x_sample.reshape(bs * ts, D_MODEL), p_sample[i].reshape(bs * ts, P_DIM),
                                pos_s, attend_sample, state_hgrn[i], w, bs, ts, False)
    k_p = jnp.transpose(k_t.reshape(1, bp, H_A, 2, DH_A, sp), (0, 1, 5, 2, 3, 4))

    return (y_p.reshape(bp, sp, D_MODEL), y_s.reshape(bs, ts, D_MODEL),
            k_p, v_p.reshape(1, bp, sp, H_A, E_A), s_p.reshape(1, bp, H_R, K_R, V_R),
            k_s.reshape(1, bs, ts, H_A, 2, DH_A), v_s.reshape(1, bs, ts, H_A, E_A), s_s.reshape(1, bs, H_R, K_R, V_R))
```

```python
import functools
import math

import jax
import jax.numpy as jnp
import numpy as np
from jax import lax
from jax.experimental import pallas as pl
from jax.experimental.pallas import tpu as pltpu

F32 = jnp.float32
BF16 = jnp.bfloat16
I32 = jnp.int32

D_MODEL = 1024
P_DIM = 256
H_A = 4
DH_A = 64
E_A = 2 * DH_A
H_R = 4
K_R = 128
V_R = 128
HGRN_CHUNK = 32
N_GROUPS = 4
EXPERTS_PER_GROUP = 4
N_EXPERTS = N_GROUPS * EXPERTS_PER_GROUP
D_EXPERT = 256
ROPE_THETA = 10000.0
EPS = 1e-6
NEG_INF = -1e30
LOG2_E = 1.4426950408889634
PAGE_SIZE = 128
ATT_W = H_A * E_A
REC_W = H_R * V_R
N_SEQ_COLS = 7 * 512
LANES = 128
N_PAIRS = 6
N_CLASSES = N_GROUPS * N_PAIRS
PAIR_LO = (0, 0, 0, 1, 1, 2)
PAIR_HI = (1, 2, 3, 2, 3, 3)
ROW_W = D_MODEL + LANES
TOKEN_TILE = 256
INPROJ_TILE = 512
POSITION_TILES = 8
MIX_TILE = 1024
MIX_PART = 512
ATTN_TILE = 512
QUERY_GROUP = 256
SCORE_LOOKAHEAD = 4
PAGES_PER_STEP = 16
VMEM_LIMIT = 56 * 1024 * 1024


def _cparams(n_axes, flags=None):
    return pltpu.CompilerParams(dimension_semantics=("arbitrary",) * n_axes, vmem_limit_bytes=VMEM_LIMIT, flags=flags)


def _rms(x, g):
    return x * lax.rsqrt(jnp.mean(x * x, axis=-1, keepdims=True) + EPS) * g


def _silu(x):
    return x * jax.nn.sigmoid(x)


def _dot(a, b):
    return jnp.dot(a, b, preferred_element_type=F32)


def _dot_nt(a, b):
    return lax.dot_general(a, b, (((1,), (1,)), ((), ())), preferred_element_type=F32)


def _inproj_kernel(x_ref, g_ref, w_ref, cos_ref, sin_ref, *out_refs, transposed):
    h = _rms(x_ref[...], g_ref[...]).astype(BF16)
    cos = cos_ref[...]
    sin = sin_ref[...]
    tm = h.shape[0]
    lane = lax.broadcasted_iota(I32, (tm, ATT_W), 1)
    first_half = (lane % DH_A) < (DH_A // 2)

    def proj(c):
        return _dot(h, w_ref[:, c * 512:(c + 1) * 512])

    def rope(t):
        rot = jnp.where(first_half, pltpu.roll(t, ATT_W - DH_A // 2, 1), pltpu.roll(t, DH_A // 2, 1))
        return t * cos + rot * sin

    q = rope(proj(0)) * (DH_A ** -0.5)
    k = rope(proj(1))
    v = proj(2)
    if transposed:
        qt_ref, kt_ref, kb_ref, vf_ref, vt_ref = out_refs[:5]
        qt_ref[...] = (q * LOG2_E).T.astype(BF16)
        kt_ref[...] = k.T
        kb_ref[...] = k.astype(BF16)
        for head in range(H_A):
            vf_ref[pl.ds(head, tm, stride=H_A), :] = v[:, head * E_A:(head + 1) * E_A]
        vt_ref[...] = v.T.astype(BF16)
        rest = out_refs[5:]
    else:
        q_ref, kf_ref, vf_ref = out_refs[:3]
        q_ref[...] = q
        kf_ref[...] = k
        vf_ref[...] = v
        rest = out_refs[3:]
    qr_ref, fl_ref, ir_ref, gr_ref = rest
    qr_ref[...] = _silu(proj(3))
    fl_ref[...] = proj(4)
    ir_ref[...] = proj(5)
    gr_ref[...] = proj(6)


def _inproj(x, g_mix, w_seq, cos_t, sin_t, batch, seq, transposed):
    n = x.shape[0]
    tm = cos_t.shape[0] if seq < TOKEN_TILE else min(INPROJ_TILE, seq)
    nblk = max(seq // tm, 1)
    row = lambda i: (i, 0)
    tab = lambda i: (i % nblk, 0)
    const = lambda i: (0, 0)
    o512 = pl.BlockSpec((tm, 512), row)
    sds = lambda dt: jax.ShapeDtypeStruct((n, 512), dt)
    if transposed:
        ot = pl.BlockSpec((None, 512, tm), lambda i: (i // nblk, 0, i % nblk))
        sdt = lambda dt: jax.ShapeDtypeStruct((batch, 512, seq), dt)
        ov = pl.BlockSpec((tm * H_A, E_A), row)
        out_specs = [ot, ot, o512, ov, ot] + [o512] * 4
        out_shape = ([sdt(BF16), sdt(F32), sds(BF16), jax.ShapeDtypeStruct((n * H_A, E_A), F32), sdt(BF16)]
                     + [sds(F32)] * 4)
    else:
        out_specs = [o512] * 7
        out_shape = [sds(F32)] * 7
    return pl.pallas_call(
        functools.partial(_inproj_kernel, transposed=transposed),
        grid=(n // tm,),
        in_specs=[pl.BlockSpec((tm, D_MODEL), row), pl.BlockSpec((1, D_MODEL), const),
                  pl.BlockSpec((D_MODEL, N_SEQ_COLS), const),
                  pl.BlockSpec((tm, ATT_W), tab), pl.BlockSpec((tm, ATT_W), tab)],
        out_specs=out_specs,
        out_shape=out_shape,
        compiler_params=_cparams(1),
        name="inproj",
    )(x, g_mix, w_seq, cos_t, sin_t)


def _lambda_value(lam_ref):
    lam = lam_ref[...]
    s01 = jnp.sum(lam[0:1, :] * lam[1:2, :], axis=-1, keepdims=True)
    s23 = jnp.sum(lam[2:3, :] * lam[3:4, :], axis=-1, keepdims=True)
    return jnp.exp(s01) - jnp.exp(s23)


def _diff_norm(o0, o1, lam_val, g_sub, lam_init):
    od = o0 - lam_val * o1
    return _rms(od, g_sub) * (1.0 - lam_init)


def _split_maps(qh):
    lane = lax.broadcasted_iota(I32, qh.shape, 1)
    zero = jnp.zeros_like(qh)
    return jnp.concatenate([jnp.where(lane < DH_A, qh, zero), jnp.where(lane >= DH_A, qh, zero)], axis=0)


def _attn_prompt_kernel(q_ref, k_ref, v_ref, lam_ref, g_ref, o_ref, q2_scr, *state, lam_init):
    i = pl.program_id(1)
    tq = q_ref.shape[1]
    chains = [(h, c0) for h in range(H_A) for c0 in range(0, 2 * tq, QUERY_GROUP)]
    m_scrs, l_scrs, acc_scrs = state[0::3], state[1::3], state[2::3]
    feat = lax.broadcasted_iota(I32, (E_A, tq), 0)
    for h in range(H_A):
        qh = q_ref[h * E_A:(h + 1) * E_A, :].astype(F32)
        q2_scr[h] = jnp.concatenate([jnp.where(feat < DH_A, qh, 0.0), jnp.where(feat >= DH_A, qh, 0.0)],
                                    axis=1).astype(BF16)
    for n in range(len(chains)):
        m_scrs[n][...] = jnp.full(m_scrs[n].shape, NEG_INF, F32)
        l_scrs[n][...] = jnp.zeros(l_scrs[n].shape, F32)
        acc_scrs[n][...] = jnp.zeros(acc_scrs[n].shape, F32)

    def key_blocks(blocks):
        starts = [pl.multiple_of(kb * tq, tq) for kb, _ in blocks]
        items = [(b, n) for b in range(len(blocks)) for n in range(len(chains))]

        def n_keys(b, c0):
            return min(tq, c0 % tq + QUERY_GROUP) if blocks[b][1] else tq

        def scores(item):
            b, n = item
            h, c0 = chains[n]
            nk = n_keys(b, c0)
            s = _dot(k_ref[pl.ds(starts[b], nk), h * E_A:(h + 1) * E_A], q2_scr[h, :, c0:c0 + QUERY_GROUP])
            if blocks[b][1]:
                key = lax.broadcasted_iota(I32, (nk, QUERY_GROUP), 0)
                qry = (lax.broadcasted_iota(I32, (nk, QUERY_GROUP), 1) + c0) % tq
                s = jnp.where(key > qry, NEG_INF, s)
            return s

        ahead = [scores(it) for it in items[:SCORE_LOOKAHEAD]]
        for w, (b, n) in enumerate(items):
            h, c0 = chains[n]
            cs = slice(h * E_A, (h + 1) * E_A)
            s = ahead.pop(0)
            if w + SCORE_LOOKAHEAD < len(items):
                ahead.append(scores(items[w + SCORE_LOOKAHEAD]))
            m_old = m_scrs[n][...]
            m_new = jnp.maximum(m_old, jnp.max(s, axis=0, keepdims=True))
            alpha = jnp.exp2(m_old - m_new)
            p = jnp.exp2(s - m_new)
            l_scrs[n][...] = l_scrs[n][...] * alpha + jnp.sum(p, axis=0, keepdims=True)
            acc_scrs[n][...] = (acc_scrs[n][...] * alpha
                                + _dot(v_ref[cs, pl.ds(starts[b], n_keys(b, c0))], p.astype(BF16)))
            m_scrs[n][...] = m_new

    def below_diagonal_pair(jj, carry):
        key_blocks([(2 * jj, False), (2 * jj + 1, False)])
        return carry

    lax.fori_loop(0, i // 2, below_diagonal_pair, 0)

    @pl.when(i % 2 == 1)
    def _():
        key_blocks([(i - 1, False), (i, True)])

    @pl.when(i % 2 == 0)
    def _():
        key_blocks([(i, True)])

    lam_val = _lambda_value(lam_ref) + lam_init
    per_head = len(chains) // H_A
    for h in range(H_A):
        o = jnp.concatenate([acc_scrs[n][...] / l_scrs[n][...] for n in range(h * per_head, (h + 1) * per_head)],
                            axis=1)
        od = o[:, :tq] - lam_val * o[:, tq:]
        ms = jnp.mean(od * od, axis=0, keepdims=True)
        y_t = od * lax.rsqrt(ms + EPS) * g_ref[...] * (1.0 - lam_init)
        o_ref[:, h * E_A:(h + 1) * E_A] = y_t.T.astype(o_ref.dtype)


def _attn_prompt(q_t, k, v_t, lam, g_sub_col, batch, seq, lam_init):
    tq = min(ATTN_TILE, seq)
    nq = seq // tq
    n_chains = H_A * (2 * tq // QUERY_GROUP)
    state = [pltpu.VMEM((1, QUERY_GROUP), F32), pltpu.VMEM((1, QUERY_GROUP), F32),
             pltpu.VMEM((E_A, QUERY_GROUP), F32)] * n_chains
    return pl.pallas_call(
        functools.partial(_attn_prompt_kernel, lam_init=lam_init),
        grid=(batch, nq),
        in_specs=[pl.BlockSpec((None, ATT_W, tq), lambda b, i: (b, 0, i)),
                  pl.BlockSpec((seq, ATT_W), lambda b, i: (b, 0)),
                  pl.BlockSpec((None, ATT_W, seq), lambda b, i: (b, 0, 0)),
                  pl.BlockSpec((4, DH_A), lambda b, i: (0, 0)),
                  pl.BlockSpec((E_A, 1), lambda b, i: (0, 0))],
        out_specs=pl.BlockSpec((tq, ATT_W), lambda b, i: (b * nq + i, 0)),
        out_shape=jax.ShapeDtypeStruct(k.shape, BF16),
        scratch_shapes=[pltpu.VMEM((H_A, E_A, 2 * tq), BF16)] + state,
        compiler_params=_cparams(2),
        name="attn_prompt",
    )(q_t, k, v_t, lam, g_sub_col)


def _attn_sample_kernel(pt_ref, q_ref, kn_ref, vn_ref, lam_ref, g_ref, *refs, lam_init, pages):
    k_refs = refs[:pages]
    v_refs = refs[pages:2 * pages]
    o_ref = refs[2 * pages]
    m_scr, l_scr, acc_scr = refs[2 * pages + 1:]
    j = pl.program_id(1)
    t = q_ref.shape[0]

    @pl.when(j == 0)
    def _():
        m_scr[...] = jnp.full(m_scr.shape, NEG_INF, F32)
        l_scr[...] = jnp.zeros(l_scr.shape, F32)
        acc_scr[...] = jnp.zeros(acc_scr.shape, F32)

    rows_h = 2 * t

    def own_head_blocks(big):
        return jnp.concatenate([big[h * rows_h:(h + 1) * rows_h, h * E_A:(h + 1) * E_A] for h in range(H_A)], axis=0)

    def update(s, v_all):
        m_old = m_scr[...]
        m_new = jnp.maximum(m_old, jnp.max(s, axis=-1, keepdims=True))
        alpha = jnp.exp(m_old - m_new)
        p = jnp.exp(s - m_new)
        l_scr[...] = l_scr[...] * alpha + jnp.sum(p, axis=-1, keepdims=True)
        acc_scr[...] = acc_scr[...] * alpha + own_head_blocks(_dot(p.astype(BF16), v_all))
        m_scr[...] = m_new

    q = q_ref[...]
    zero = jnp.zeros((rows_h, E_A), F32)
    qall = jnp.concatenate(
        [jnp.concatenate([_split_maps(q[:, h * E_A:(h + 1) * E_A]) if hh == h else zero for hh in range(H_A)], axis=1)
         for h in range(H_A)], axis=0).astype(BF16)

    kt = jnp.concatenate([k_refs[r][...].astype(BF16) for r in range(pages)], axis=1)
    v_all = jnp.concatenate(
        [jnp.concatenate([v_refs[r][pl.ds(h, PAGE_SIZE, stride=H_A), :] for h in range(H_A)], axis=1)
         for r in range(pages)], axis=0).astype(BF16)
    update(_dot(qall, kt), v_all)

    @pl.when(j == pl.num_programs(1) - 1)
    def _():
        lam_val = _lambda_value(lam_ref) + lam_init
        n_rows = H_A * rows_h
        row_t = lax.broadcasted_iota(I32, (n_rows, PAGE_SIZE), 0) % t
        col = lax.broadcasted_iota(I32, (n_rows, PAGE_SIZE), 1)
        visible = col <= row_t
        pad = jnp.zeros((PAGE_SIZE - t, ATT_W), F32)
        kn_t = jnp.concatenate([kn_ref[...], pad], axis=0).T.astype(BF16)
        vn = jnp.concatenate([vn_ref[...], pad], axis=0).astype(BF16)
        update(jnp.where(visible, _dot(qall, kn_t), NEG_INF), vn)
        o = acc_scr[...] / l_scr[...]
        for h in range(H_A):
            o0 = o[h * rows_h:h * rows_h + t]
            o1 = o[h * rows_h + t:(h + 1) * rows_h]
            y = _diff_norm(o0, o1, lam_val, g_ref[...], lam_init)
            o_ref[:, h * E_A:(h + 1) * E_A] = y.astype(BF16).astype(o_ref.dtype)


def _attn_sample(q, kn, vn, cache_kt, cache_v2, page_table, lam, g_sub, batch, t, lam_init):
    pages = PAGES_PER_STEP
    n_pages = page_table.shape[1]
    steps = n_pages // pages
    fixed = lambda b, j, pt: (0, 0)
    rowb = lambda b, j, pt: (b, 0)
    n_rows = H_A * 2 * t

    def page_spec(r):
        return pl.BlockSpec((None, ATT_W, PAGE_SIZE), lambda b, j, pt: (pt[b, j * pages + r], 0, 0))

    grid_spec = pltpu.PrefetchScalarGridSpec(
        num_scalar_prefetch=1,
        grid=(batch, steps),
        in_specs=[pl.BlockSpec((t, ATT_W), rowb), pl.BlockSpec((t, ATT_W), rowb), pl.BlockSpec((t, ATT_W), rowb),
                  pl.BlockSpec((4, DH_A), fixed), pl.BlockSpec((1, E_A), fixed)]
                 + [page_spec(r) for r in range(pages)] + [page_spec(r) for r in range(pages)],
        out_specs=pl.BlockSpec((t, ATT_W), rowb),
        scratch_shapes=[pltpu.VMEM((n_rows, 1), F32), pltpu.VMEM((n_rows, 1), F32), pltpu.VMEM((n_rows, E_A), F32)],
    )
    return pl.pallas_call(
        functools.partial(_attn_sample_kernel, lam_init=lam_init, pages=pages),
        grid_spec=grid_spec,
        out_shape=jax.ShapeDtypeStruct(q.shape, F32),
        compiler_params=_cparams(2),
        name="attn_sample",
    )(page_table, q, kn, vn, lam, g_sub, *([cache_kt] * pages), *([cache_v2] * pages))


def _split3(x):
    a = x.astype(BF16)
    r = x - a.astype(F32)
    b = r.astype(BF16)
    c = (r - b.astype(F32)).astype(BF16)
    return a, b, c


def _hgrn_kernel(qr_ref, fl_ref, ir_ref, gr_ref, lb_ref, g_ref, s0_ref, y_ref, s_ref,
                 qe_scr, kk_scr, dl_scr, o_scr, st_scr, *, chunk, blk):
    t = qr_ref.shape[0]
    lb = lb_ref[...]
    row = lax.broadcasted_iota(I32, (blk, blk), 0)
    col = lax.broadcasted_iota(I32, (blk, blk), 1)
    same = (row // chunk) == (col // chunk)
    causal = jnp.logical_and(same, col <= row)
    cum_mask = jnp.where(causal, 1.0, 0.0).astype(BF16)
    all_mask = jnp.where(same, 1.0, 0.0).astype(BF16)

    def gates(b, carry):
        rs = pl.ds(pl.multiple_of(b * blk, blk), blk)
        f = lb + (1.0 - lb) * jax.nn.sigmoid(fl_ref[rs, :])
        g = jnp.log(f)
        k = 1.0 - f
        g1, g2, g3 = _split3(g)
        bcum = _dot(cum_mask, g1) + _dot(cum_mask, g2) + _dot(cum_mask, g3)
        blast = _dot(all_mask, g1) + _dot(all_mask, g2) + _dot(all_mask, g3)
        qe = (qr_ref[rs, :] * jnp.exp(bcum)).astype(BF16)
        kd = (k * jnp.exp(-bcum)).astype(BF16)
        qe_scr[rs, :] = qe.astype(qe_scr.dtype)
        kk_scr[rs, :] = (k * jnp.exp(blast - bcum)).astype(kk_scr.dtype)
        dl_scr[rs, :] = jnp.exp(blast)
        v = ir_ref[rs, :].astype(BF16)
        heads = [slice(h * K_R, (h + 1) * K_R) for h in range(H_R)]
        a = [jnp.where(causal, _dot_nt(qe[:, cs], kd[:, cs]), 0.0).astype(BF16) for cs in heads]
        for cs, a_h in zip(heads, a):
            o_scr[rs, cs] = _dot(a_h, v[:, cs])
        return carry

    lax.fori_loop(0, t // blk, gates, 0, unroll=2 if t // blk >= 2 else 1)

    for h in range(H_R):
        st_scr[h] = s0_ref[h].T

    n_chunks = t // chunk
    per_step = 2 if n_chunks % 2 == 0 else 1
    heads = [slice(h * K_R, (h + 1) * K_R) for h in range(H_R)]

    def step(j, carry):
        starts = [pl.multiple_of((j * per_step + u) * chunk, chunk) for u in range(per_step)]
        incs = [[_dot(ir_ref[pl.ds(r0, chunk), cs].T.astype(BF16), kk_scr[pl.ds(r0, chunk), cs].astype(BF16))
                 for cs in heads] for r0 in starts]
        for r0, inc in zip(starts, incs):
            rows = pl.ds(r0, chunk)
            for h, cs in enumerate(heads):
                st = st_scr[h]
                o_scr[rows, cs] = o_scr[rows, cs] + _dot_nt(qe_scr[rows, cs].astype(BF16), st.astype(BF16))
                st_scr[h] = st * dl_scr[pl.ds(r0, 1), cs] + inc[h]
        return carry

    lax.fori_loop(0, n_chunks // per_step, step, 0)
    for h in range(H_R):
        s_ref[h] = st_scr[h].T

    g_rec = g_ref[...]

    def finish(b, carry):
        rs = pl.ds(pl.multiple_of(b * blk, blk), blk)
        for h in range(H_R):
            cs = slice(h * K_R, (h + 1) * K_R)
            y = _rms(o_scr[rs, cs], g_rec) * _silu(gr_ref[rs, cs])
            y_ref[rs, cs] = y.astype(BF16).astype(y_ref.dtype)
        return carry

    lax.fori_loop(0, t // blk, finish, 0)


def _hgrn(qr, fl, ir, gr, lb, g_rec, s0, batch, t, out_dtype):
    chunk = math.gcd(t, HGRN_CHUNK)
    blk = min(t, 256)
    rowb = lambda b: (b, 0)
    fixed = lambda b: (0, 0)
    seq = pl.BlockSpec((t, REC_W), rowb)
    state = pl.BlockSpec((None, H_R, K_R, V_R), lambda b: (b, 0, 0, 0))
    return pl.pallas_call(
        functools.partial(_hgrn_kernel, chunk=chunk, blk=blk),
        grid=(batch,),
        in_specs=[seq, seq, seq, seq, pl.BlockSpec((1, REC_W), fixed), pl.BlockSpec((1, V_R), fixed), state],
        out_specs=[seq, state],
        out_shape=[jax.ShapeDtypeStruct(qr.shape, out_dtype), jax.ShapeDtypeStruct(s0.shape, F32)],
        scratch_shapes=[pltpu.VMEM((t, REC_W), out_dtype), pltpu.VMEM((t, REC_W), out_dtype),
                        pltpu.VMEM((t, REC_W), F32), pltpu.VMEM((t, REC_W), F32), pltpu.VMEM((H_R, V_R, K_R), F32)],
        compiler_params=_cparams(1),
        name="hgrn",
    )(qr, fl, ir, gr, lb, g_rec, s0)


def _mix_kernel(x_ref, ya_ref, yr_ref, gmix_ref, wga_ref, wgr_ref, wba_ref, wbr_ref, wout_ref,
                gffn_ref, wrt_ref, brt_ref, xr_ref, cnt_ref, cnt_scr):
    i = pl.program_id(0)
    tm = x_ref.shape[0]

    @pl.when(i == 0)
    def _():
        cnt_scr[...] = jnp.zeros(cnt_scr.shape, F32)

    part = min(tm, MIX_PART)
    parts = [slice(r, r + part) for r in range(0, tm, part)]
    xs = [x_ref[rs, :] for rs in parts]
    hs = [_rms(x, gmix_ref[...]).astype(BF16) for x in xs]
    gate_a = [jax.nn.sigmoid(_dot(h, wga_ref[...])) for h in hs]
    br_a = [_dot(ya_ref[rs, :], wba_ref[...]) for rs in parts]
    gate_r = [jax.nn.sigmoid(_dot(h, wgr_ref[...])) for h in hs]
    br_r = [_dot(yr_ref[rs, :], wbr_ref[...]) for rs in parts]
    merged = [(ga * ba + gr * br).astype(BF16) for ga, ba, gr, br in zip(gate_a, br_a, gate_r, br_r)]
    x1s = [x + _dot(m, wout_ref[...]) for x, m in zip(xs, merged)]
    for rs, x1 in zip(parts, x1s):
        xr_ref[rs, :D_MODEL] = x1
    hfs = [_rms(x1, gffn_ref[...]).astype(BF16) for x1 in x1s]
    lgs = [_dot(hf, wrt_ref[...]) + brt_ref[...] for hf in hfs]
    for rs, lg in zip(parts, lgs):
        xr_ref[rs, D_MODEL:] = _route_rows(lg, cnt_scr)
    cnt_ref[...] = cnt_scr[...]


def _route_rows(lg, cnt_scr):
    tm = lg.shape[0]
    lane = lax.broadcasted_iota(I32, (tm, LANES), 1)
    is_group = lane < N_GROUPS
    mg = jnp.max(jnp.where(is_group, lg, NEG_INF), axis=-1, keepdims=True)
    gidx = jnp.min(jnp.where(jnp.logical_and(is_group, lg == mg), lane, LANES), axis=-1, keepdims=True)
    p_top = 1.0 / jnp.sum(jnp.where(is_group, jnp.exp(lg - mg), 0.0), axis=-1, keepdims=True)
    base = N_GROUPS + EXPERTS_PER_GROUP * gidx
    in_grp = jnp.logical_and(lane >= base, lane < base + EXPERTS_PER_GROUP)
    v1 = jnp.max(jnp.where(in_grp, lg, NEG_INF), axis=-1, keepdims=True)
    e1 = jnp.min(jnp.where(jnp.logical_and(in_grp, lg == v1), lane, LANES), axis=-1, keepdims=True)
    rest = jnp.logical_and(in_grp, lane != e1)
    v2 = jnp.max(jnp.where(rest, lg, NEG_INF), axis=-1, keepdims=True)
    e2 = jnp.min(jnp.where(jnp.logical_and(rest, lg == v2), lane, LANES), axis=-1, keepdims=True)
    tt = jnp.exp(v2 - v1)
    w_a = (1.0 / (1.0 + tt)) * p_top
    w_b = (tt / (1.0 + tt)) * p_top
    a = e1 - base
    b = e2 - base
    a_first = a < b
    lo = jnp.minimum(a, b)
    hi = jnp.maximum(a, b)
    w_lo = jnp.where(a_first, w_a, w_b)
    w_hi = jnp.where(a_first, w_b, w_a)
    pair = jnp.where(lo == 0, hi - 1, jnp.where(lo == 1, hi + 1, 5))
    cls = gidx * N_PAIRS + pair

    onehot = lane == cls
    r_i = lax.broadcasted_iota(I32, (tm, tm), 0)
    c_i = lax.broadcasted_iota(I32, (tm, tm), 1)
    before = jnp.where(c_i < r_i, 1.0, 0.0).astype(BF16)
    excl = _dot(before, jnp.where(onehot, 1.0, 0.0).astype(BF16))
    rank = jnp.sum(jnp.where(onehot, excl + cnt_scr[...], 0.0), axis=-1, keepdims=True)
    cnt_scr[...] = cnt_scr[...] + jnp.sum(jnp.where(onehot, 1.0, 0.0), axis=0, keepdims=True)
    return jnp.where(lane == 0, cls.astype(F32),
                     jnp.where(lane == 1, w_lo, jnp.where(lane == 2, w_hi, jnp.where(lane == 3, rank, 0.0))))


def _mix(x, ya, yr, g_mix, wga, wgr, wba, wbr, wout, g_ffn, wrt, brt):
    n = x.shape[0]
    tm = min(MIX_TILE, n)
    row = lambda i: (i, 0)
    fixed = lambda i: (0, 0)
    full = lambda a: pl.BlockSpec(a.shape, fixed)
    return pl.pallas_call(
        _mix_kernel,
        grid=(n // tm,),
        in_specs=[pl.BlockSpec((tm, D_MODEL), row), pl.BlockSpec((tm, ATT_W), row), pl.BlockSpec((tm, REC_W), row),
                  full(g_mix), full(wga), full(wgr), full(wba), full(wbr), full(wout), full(g_ffn), full(wrt), full(brt)],
        out_specs=[pl.BlockSpec((tm, ROW_W), row), pl.BlockSpec((1, LANES), fixed)],
        out_shape=[jax.ShapeDtypeStruct((n, ROW_W), F32), jax.ShapeDtypeStruct((1, LANES), F32)],
        scratch_shapes=[pltpu.VMEM((1, LANES), F32)],
        compiler_params=_cparams(1),
        name="mix",
    )(x, ya, yr, g_mix, wga, wgr, wba, wbr, wout, g_ffn, wrt, brt)


def _row_copy_out(x_ref, o_hbm, sem, r, p):
    return pltpu.make_async_copy(x_ref.at[pl.ds(r, 1)], o_hbm.at[pl.ds(p, 1)], sem)


def _dispatch_kernel(zs_ref, pos_ref, x_ref, o_hbm, zero_scr, zsem, rsem):
    i = pl.program_id(0)
    tm = x_ref.shape[0]

    def zero_copy(c):
        start = pl.multiple_of(jnp.maximum(zs_ref[c], 0), tm)
        return pltpu.make_async_copy(zero_scr, o_hbm.at[pl.ds(start, tm)], zsem)

    @pl.when(i == 0)
    def _():
        zero_scr[...] = jnp.zeros(zero_scr.shape, F32)
        for c in range(zs_ref.shape[0]):
            @pl.when(zs_ref[c] >= 0)
            def _():
                zero_copy(c).start()
        for c in range(zs_ref.shape[0]):
            @pl.when(zs_ref[c] >= 0)
            def _():
                zero_copy(c).wait()

    for r in range(tm):
        _row_copy_out(x_ref, o_hbm, rsem, r, pos_ref[0, r]).start(priority=r % 2)
    pltpu.make_async_copy(x_ref, o_hbm.at[pl.ds(0, tm)], rsem).wait()


def _dispatch(xr, pos2d, zero_starts, n_rows_sorted):
    n = xr.shape[0]
    tm = TOKEN_TILE
    grid_spec = pltpu.PrefetchScalarGridSpec(
        num_scalar_prefetch=1,
        grid=(n // tm,),
        in_specs=[pl.BlockSpec((None, 1, tm), lambda i, zs: (i, 0, 0), memory_space=pltpu.SMEM),
                  pl.BlockSpec((tm, ROW_W), lambda i, zs: (i, 0))],
        out_specs=pl.BlockSpec(memory_space=pl.ANY),
        scratch_shapes=[pltpu.VMEM((tm, ROW_W), F32), pltpu.SemaphoreType.DMA(()), pltpu.SemaphoreType.DMA(())],
    )
    return pl.pallas_call(
        _dispatch_kernel,
        grid_spec=grid_spec,
        out_shape=jax.ShapeDtypeStruct((n_rows_sorted, ROW_W), F32),
        compiler_params=_cparams(1),
        name="dispatch",
    )(zero_starts, pos2d, xr)


MOE_TILES_PER_STEP = 2


def _moe_kernel(blk_ref, elo_ref, ehi_ref, valid_ref, g_ref, *refs):
    j = pl.program_id(0)
    k = MOE_TILES_PER_STEP
    xs_refs = refs[:k]
    w_refs = [refs[k + 6 * u:k + 6 * (u + 1)] for u in range(k)]
    o_ref = refs[7 * k]
    tm = xs_refs[0].shape[0]
    xs = [r[:, :D_MODEL] for r in xs_refs]
    hs = [_rms(x, g_ref[...]).astype(BF16) for x in xs]
    a_lo = [_dot(h, w[0][...]) for h, w in zip(hs, w_refs)]
    u_lo = [_dot(h, w[2][...]) for h, w in zip(hs, w_refs)]
    a_hi = [_dot(h, w[1][...]) for h, w in zip(hs, w_refs)]
    u_hi = [_dot(h, w[3][...]) for h, w in zip(hs, w_refs)]
    hid_lo = [(_silu(a) * u * r[:, D_MODEL + 1:D_MODEL + 2]).astype(BF16) for a, u, r in zip(a_lo, u_lo, xs_refs)]
    hid_hi = [(_silu(a) * u * r[:, D_MODEL + 2:D_MODEL + 3]).astype(BF16) for a, u, r in zip(a_hi, u_hi, xs_refs)]
    ys = [_dot(hl, w[4][...]) + _dot(hh, w[5][...]) for hl, hh, w in zip(hid_lo, hid_hi, w_refs)]
    for u in range(k):
        o_ref[u * tm:(u + 1) * tm, :] = jnp.where(valid_ref[k * j + u] == 1, xs[u] + ys[u], 0.0)


def _moe(xs, g_ffn, wg, wu, wd, in_blk, e_lo, e_hi, valid):
    tm = TOKEN_TILE
    k = MOE_TILES_PER_STEP
    n_tiles = xs.shape[0] // tm

    def tile_specs(u):
        lo = lambda j, blk, elo, ehi, va: (elo[k * j + u], 0, 0)
        hi = lambda j, blk, elo, ehi, va: (ehi[k * j + u], 0, 0)
        up = pl.BlockSpec((None, D_MODEL, D_EXPERT), lo), pl.BlockSpec((None, D_MODEL, D_EXPERT), hi)
        down = pl.BlockSpec((None, D_EXPERT, D_MODEL), lo), pl.BlockSpec((None, D_EXPERT, D_MODEL), hi)
        return [up[0], up[1], up[0], up[1], down[0], down[1]]

    def rows_spec(u):
        return pl.BlockSpec((tm, ROW_W), lambda j, blk, elo, ehi, va: (blk[k * j + u], 0))

    grid_spec = pltpu.PrefetchScalarGridSpec(
        num_scalar_prefetch=4,
        grid=(n_tiles // k,),
        in_specs=[pl.BlockSpec((1, D_MODEL), lambda j, blk, elo, ehi, va: (0, 0))]
                 + [rows_spec(u) for u in range(k)] + [s for u in range(k) for s in tile_specs(u)],
        out_specs=pl.BlockSpec((k * tm, D_MODEL), lambda j, blk, elo, ehi, va: (j, 0)),
    )
    return pl.pallas_call(
        _moe_kernel,
        grid_spec=grid_spec,
        out_shape=jax.ShapeDtypeStruct((xs.shape[0], D_MODEL), F32),
        compiler_params=_cparams(1),
        name="moe",
    )(in_blk, e_lo, e_hi, valid, g_ffn, *([xs] * k), *([wg, wg, wu, wu, wd, wd] * k))


def _ple_kernel(pos_ref, pos_next_ref, xs_hbm, p_ref, gple_ref, wpg_ref, wp_ref, gfin_ref, y_ref, buf, sem):
    i = pl.program_id(0)
    tm = buf.shape[1]
    slot = i % 2

    def gather(idx_ref, s):
        for r in range(tm):
            pltpu.make_async_copy(xs_hbm.at[pl.ds(idx_ref[0, r], 1)], buf.at[s, pl.ds(r, 1)],
                                  sem.at[s]).start(priority=r % 2)

    @pl.when(i == 0)
    def _():
        gather(pos_ref, 0)

    @pl.when(i + 1 < pl.num_programs(0))
    def _():
        gather(pos_next_ref, 1 - slot)

    pltpu.make_async_copy(xs_hbm.at[pl.ds(0, tm)], buf.at[slot], sem.at[slot]).wait()

    x2 = buf[slot]
    hp = _rms(x2, gple_ref[...]).astype(BF16)
    gate = jax.nn.sigmoid(_dot(hp, wpg_ref[...]))
    x3 = x2 + gate * _dot(p_ref[...].astype(BF16), wp_ref[...])
    y_ref[...] = _rms(x3, gfin_ref[...])


def _ple(xs2, pos2d, p, g_ple, wpg, wp, g_final):
    n = p.shape[0]
    tm = TOKEN_TILE
    row = lambda i: (i, 0)
    fixed = lambda i: (0, 0)
    last = n // tm - 1
    return pl.pallas_call(
        _ple_kernel,
        grid=(n // tm,),
        in_specs=[pl.BlockSpec((None, 1, tm), lambda i: (i, 0, 0), memory_space=pltpu.SMEM),
                  pl.BlockSpec((None, 1, tm), lambda i: (jnp.minimum(i + 1, last), 0, 0), memory_space=pltpu.SMEM),
                  pl.BlockSpec(memory_space=pl.ANY),
                  pl.BlockSpec((tm, P_DIM), row), pl.BlockSpec((1, D_MODEL), fixed),
                  pl.BlockSpec((D_MODEL, D_MODEL), fixed), pl.BlockSpec((P_DIM, D_MODEL), fixed),
                  pl.BlockSpec((1, D_MODEL), fixed)],
        out_specs=pl.BlockSpec((tm, D_MODEL), row),
        out_shape=jax.ShapeDtypeStruct((n, D_MODEL), F32),
        scratch_shapes=[pltpu.VMEM((2, tm, D_MODEL), F32), pltpu.SemaphoreType.DMA((2,))],
        compiler_params=_cparams(1),
        name="ple",
    )(pos2d, pos2d, xs2, p, g_ple, wpg, wp, g_final)


def _positions_kernel(off_ref, route_ref, pos_ref):
    n_sub, _, tm = pos_ref.shape
    lane = lax.broadcasted_iota(I32, (tm, LANES), 1)
    for u in range(n_sub):
        route = route_ref[u * tm:(u + 1) * tm, :]
        cls = route[:, 0:1].astype(I32)
        first = jnp.sum(jnp.where(lane == cls, off_ref[...], 0.0), axis=-1, keepdims=True)
        pos = first + route[:, 3:4]
        pos_ref[u] = jnp.where(lane == 0, pos, 0.0).T[0:1, :].astype(I32)


def _positions(xr, class_row_start, n):
    tm = TOKEN_TILE
    n_sub = min(POSITION_TILES, n // tm)
    return pl.pallas_call(
        _positions_kernel,
        grid=(n // (tm * n_sub),),
        in_specs=[pl.BlockSpec((1, LANES), lambda i: (0, 0)),
                  pl.BlockSpec((tm * n_sub, LANES), lambda i: (i, D_MODEL // LANES))],
        out_specs=pl.BlockSpec((n_sub, 1, tm), lambda i: (i, 0, 0)),
        out_shape=jax.ShapeDtypeStruct((n // tm, 1, tm), I32),
        compiler_params=_cparams(1),
        name="positions",
    )(class_row_start, xr)


def _routing_plan(xr, counts, n):
    tm = TOKEN_TILE
    n_tiles = pl.cdiv(n // tm + N_CLASSES, MOE_TILES_PER_STEP) * MOE_TILES_PER_STEP
    cnt = counts[0, :N_CLASSES].astype(I32)
    tiles = (cnt + tm - 1) // tm
    tile_end = jnp.cumsum(tiles)
    tile_start = tile_end - tiles
    n_used = tile_end[-1]
    class_row_start = jnp.pad((tile_start * tm).astype(F32), (0, LANES - N_CLASSES)).reshape(1, LANES)
    pos3d = _positions(xr, class_row_start, n)
    j = jnp.arange(n_tiles, dtype=I32)
    valid = j < n_used
    in_blk = jnp.minimum(j, n_used - 1)
    tcls = jnp.sum((tile_end[None, :] <= in_blk[:, None]).astype(I32), axis=1)
    grp = tcls // N_PAIRS
    pair = tcls % N_PAIRS
    e_lo = grp * EXPERTS_PER_GROUP + jnp.asarray(PAIR_LO, I32)[pair]
    e_hi = grp * EXPERTS_PER_GROUP + jnp.asarray(PAIR_HI, I32)[pair]
    seg_zero = jnp.where(tiles > 0, tile_end * tm - tm, -1)
    tail = n_used + jnp.arange(n_tiles - n // tm, dtype=I32)
    tail_zero = jnp.where(tail < n_tiles, tail * tm, -1)
    zero_starts = jnp.concatenate([seg_zero, tail_zero]).astype(I32)
    return pos3d, zero_starts, in_blk, e_lo, e_hi, valid.astype(I32), n_tiles * tm


def _rope_tables(pos, rows):
    half = DH_A // 2
    inv_freq = ROPE_THETA ** (-jnp.arange(half, dtype=F32) / half)
    ang = pos.astype(F32)[:, None] * inv_freq[None, :]
    cos = jnp.cos(ang)
    sin = jnp.sin(ang)
    cos_t = jnp.tile(jnp.concatenate([cos, cos], axis=-1), (1, ATT_W // DH_A))
    sin_t = jnp.tile(jnp.concatenate([-sin, sin], axis=-1), (1, ATT_W // DH_A))
    reps = max(rows // pos.shape[0], 1)
    return jnp.tile(cos_t, (reps, 1)), jnp.tile(sin_t, (reps, 1))


def _layer(i, x, p, pos, attend, s0, w, batch, seq, transposed):
    n = batch * seq
    small = BF16 if seq % 16 == 0 else F32
    cos_t, sin_t = _rope_tables(pos, TOKEN_TILE)
    proj = _inproj(x, w["g_mix"], w["w_seq"], cos_t, sin_t, batch, seq, transposed)
    lam_init = 0.8 - 0.6 * math.exp(-0.3 * i)
    ya, kf, vf = attend(proj, lam_init)
    qr, fl, ir, gr = proj[-4:]
    yr, s_new = _hgrn(qr, fl, ir, gr, w["lb"], w["g_rec"], s0, batch, seq, small)
    xr, counts = _mix(x, ya.astype(BF16), yr.astype(BF16), w["g_mix"], w["w_gate_a"], w["w_gate_r"], w["w_branch_a"],
                      w["w_branch_r"], w["w_out"], w["g_ffn"], w["w_route"], w["b_route"])
    pos2d, zero_starts, in_blk, e_lo, e_hi, valid, n_sorted = _routing_plan(xr, counts, n)
    xs = _dispatch(xr, pos2d, zero_starts, n_sorted)
    xs2 = _moe(xs, w["g_ffn"], w["w_exp_gate"], w["w_exp_up"], w["w_exp_down"], in_blk, e_lo, e_hi, valid)
    y = _ple(xs2, pos2d, p, w["g_ple"], w["w_ple_gate"], w["w_ple"], w["g_final"])
    return y, kf, vf, s_new


def kernel(x_prompt, x_sample, p_prompt, p_sample, cache_k, cache_v, state_hgrn, page_table, g_mix, w_in, lam,
           g_subln, lb_param, g_rec, w_branch_a, w_branch_r, w_out, g_ffn, w_route_group, b_route_group,
           w_route_expert, b_route_expert, w_exp_gate, w_exp_up, w_exp_down, g_ple, w_ple_gate, w_ple, g_final):
    depth = w_in.shape[0]
    assert depth == 1, "single-layer step"
    bp, sp, _ = x_prompt.shape
    bs, ts, _ = x_sample.shape
    past_len = page_table.shape[1] * PAGE_SIZE
    i = 0

    w_in_b = w_in[i].astype(BF16)
    n_route = N_GROUPS + N_EXPERTS
    w_route = jnp.concatenate([w_route_group[i], w_route_expert[i]], axis=1)
    w_route = jnp.pad(w_route, ((0, 0), (0, LANES - n_route))).astype(BF16)
    b_route = jnp.pad(jnp.concatenate([b_route_group[i], b_route_expert[i]]), (0, LANES - n_route)).reshape(1, LANES)
    lb = jnp.cumsum(jax.nn.softmax(lb_param.astype(F32), axis=0), axis=0)[i].reshape(1, REC_W)
    w = dict(
        g_mix=g_mix[i].reshape(1, D_MODEL), w_seq=w_in_b[:, :N_SEQ_COLS],
        w_gate_a=w_in_b[:, N_SEQ_COLS:N_SEQ_COLS + D_MODEL], w_gate_r=w_in_b[:, N_SEQ_COLS + D_MODEL:],
        lb=lb, g_rec=g_rec[i].reshape(1, V_R),
        w_branch_a=w_branch_a[i].astype(BF16), w_branch_r=w_branch_r[i].astype(BF16), w_out=w_out[i].astype(BF16),
        g_ffn=g_ffn[i].reshape(1, D_MODEL), w_route=w_route, b_route=b_route,
        w_exp_gate=w_exp_gate[i].astype(BF16), w_exp_up=w_exp_up[i].astype(BF16), w_exp_down=w_exp_down[i].astype(BF16),
        g_ple=g_ple[i].reshape(1, D_MODEL), w_ple_gate=w_ple_gate[i].astype(BF16), w_ple=w_ple[i].astype(BF16),
        g_final=g_final.reshape(1, D_MODEL),
    )
    lam_i = lam[i].astype(F32)
    g_sub = g_subln[i].reshape(1, E_A)

    def attend_prompt(proj, lam_init):
        q_t, k_t, kb, vf, v_t = proj[:5]
        return _attn_prompt(q_t, kb, v_t, lam_i, g_sub.reshape(E_A, 1), bp, sp, lam_init), k_t, vf

    n_phys = cache_k.shape[1]
    ckt = jnp.transpose(cache_k[i], (0, 2, 3, 4, 1)).reshape(n_phys, ATT_W, PAGE_SIZE)
    cv2 = cache_v[i].reshape(n_phys, PAGE_SIZE * H_A, E_A)

    def attend_sample(proj, lam_init):
        q, kf, vf = proj[:3]
        return _attn_sample(q, kf, vf, ckt, cv2, page_table, lam_i, g_sub, bs, ts, lam_init), kf, vf

    pos_p = jnp.arange(sp, dtype=I32)
    pos_s = past_len + jnp.arange(ts, dtype=I32)
    s0_p = jnp.zeros((bp, H_R, K_R, V_R), F32)

    y_p, k_t, v_p, s_p = _layer(i, x_prompt.reshape(bp * sp, D_MODEL), p_prompt[i].reshape(bp * sp, P_DIM),
                                pos_p, attend_prompt, s0_p, w, bp, sp, True)
    y_s, k_s, v_s, s_s = _layer(i, x_sample.reshape(bs * ts, D_MODEL), p_sample[i].reshape(bs * ts, P_DIM),
                                pos_s, attend_sample, state_hgrn[i], w, bs, ts, False)
    k_p = jnp.transpose(k_t.reshape(1, bp, H_A, 2, DH_A, sp), (0, 1, 5, 2, 3, 4))

    return (y_p.reshape(bp, sp, D_MODEL), y_s.reshape(bs, ts, D_MODEL),
            k_p, v_p.reshape(1, bp, sp, H_A, E_A), s_p.reshape(1, bp, H_R, K_R, V_R),
            k_s.reshape(1, bs, ts, H_A, 2, DH_A), v_s.reshape(1, bs, ts, H_A, E_A), s_s.reshape(1, bs, H_R, K_R, V_R))
```

```python
import functools
import math

import jax
import jax.numpy as jnp
import numpy as np
from jax import lax
from jax.experimental import pallas as pl
from jax.experimental.pallas import tpu as pltpu

F32 = jnp.float32
BF16 = jnp.bfloat16
I32 = jnp.int32

D_MODEL = 1024
P_DIM = 256
H_A = 4
DH_A = 64
E_A = 2 * DH_A
H_R = 4
K_R = 128
V_R = 128
HGRN_CHUNK = 32
N_GROUPS = 4
EXPERTS_PER_GROUP = 4
N_EXPERTS = N_GROUPS * EXPERTS_PER_GROUP
D_EXPERT = 256
ROPE_THETA = 10000.0
EPS = 1e-6
NEG_INF = -1e30
LOG2_E = 1.4426950408889634
PAGE_SIZE = 128
ATT_W = H_A * E_A
REC_W = H_R * V_R
N_SEQ_COLS = 7 * 512
LANES = 128
N_PAIRS = 6
N_CLASSES = N_GROUPS * N_PAIRS
PAIR_LO = (0, 0, 0, 1, 1, 2)
PAIR_HI = (1, 2, 3, 2, 3, 3)
ROW_W = D_MODEL + LANES
TOKEN_TILE = 256
INPROJ_TILE = 512
POSITION_TILES = 8
MIX_TILE = 1024
MIX_PART = 512
ATTN_TILE = 512
QUERY_GROUP = 256
SCORE_LOOKAHEAD = 4
PAGES_PER_STEP = 32
VMEM_LIMIT = 56 * 1024 * 1024


def _cparams(n_axes, flags=None):
    return pltpu.CompilerParams(dimension_semantics=("arbitrary",) * n_axes, vmem_limit_bytes=VMEM_LIMIT, flags=flags)


def _rms(x, g):
    return x * lax.rsqrt(jnp.mean(x * x, axis=-1, keepdims=True) + EPS) * g


def _silu(x):
    return x * jax.nn.sigmoid(x)


def _dot(a, b):
    return jnp.dot(a, b, preferred_element_type=F32)


def _dot_nt(a, b):
    return lax.dot_general(a, b, (((1,), (1,)), ((), ())), preferred_element_type=F32)


def _inproj_kernel(x_ref, g_ref, w_ref, cos_ref, sin_ref, *out_refs, transposed):
    h = _rms(x_ref[...], g_ref[...]).astype(BF16)
    cos = cos_ref[...]
    sin = sin_ref[...]
    tm = h.shape[0]
    lane = lax.broadcasted_iota(I32, (tm, ATT_W), 1)
    first_half = (lane % DH_A) < (DH_A // 2)

    def proj(c):
        return _dot(h, w_ref[:, c * 512:(c + 1) * 512])

    def rope(t):
        rot = jnp.where(first_half, pltpu.roll(t, ATT_W - DH_A // 2, 1), pltpu.roll(t, DH_A // 2, 1))
        return t * cos + rot * sin

    q = rope(proj(0)) * (DH_A ** -0.5)
    k = rope(proj(1))
    v = proj(2)
    if transposed:
        qt_ref, kt_ref, kb_ref, vf_ref, vt_ref = out_refs[:5]
        qt_ref[...] = (q * LOG2_E).T.astype(BF16)
        kt_ref[...] = k.T
        kb_ref[...] = k.astype(BF16)
        for head in range(H_A):
            vf_ref[pl.ds(head, tm, stride=H_A), :] = v[:, head * E_A:(head + 1) * E_A]
        vt_ref[...] = v.T.astype(BF16)
        rest = out_refs[5:]
    else:
        q_ref, kf_ref, vf_ref = out_refs[:3]
        q_ref[...] = q
        kf_ref[...] = k
        vf_ref[...] = v
        rest = out_refs[3:]
    qr_ref, fl_ref, ir_ref, gr_ref = rest
    qr_ref[...] = _silu(proj(3))
    fl_ref[...] = proj(4)
    ir_ref[...] = proj(5)
    gr_ref[...] = proj(6)


def _inproj(x, g_mix, w_seq, cos_t, sin_t, batch, seq, transposed):
    n = x.shape[0]
    tm = cos_t.shape[0] if seq < TOKEN_TILE else min(INPROJ_TILE, seq)
    nblk = max(seq // tm, 1)
    row = lambda i: (i, 0)
    tab = lambda i: (i % nblk, 0)
    const = lambda i: (0, 0)
    o512 = pl.BlockSpec((tm, 512), row)
    sds = lambda dt: jax.ShapeDtypeStruct((n, 512), dt)
    if transposed:
        ot = pl.BlockSpec((None, 512, tm), lambda i: (i // nblk, 0, i % nblk))
        sdt = lambda dt: jax.ShapeDtypeStruct((batch, 512, seq), dt)
        ov = pl.BlockSpec((tm * H_A, E_A), row)
        out_specs = [ot, ot, o512, ov, ot] + [o512] * 4
        out_shape = ([sdt(BF16), sdt(F32), sds(BF16), jax.ShapeDtypeStruct((n * H_A, E_A), F32), sdt(BF16)]
                     + [sds(F32)] * 4)
    else:
        out_specs = [o512] * 7
        out_shape = [sds(F32)] * 7
    return pl.pallas_call(
        functools.partial(_inproj_kernel, transposed=transposed),
        grid=(n // tm,),
        in_specs=[pl.BlockSpec((tm, D_MODEL), row), pl.BlockSpec((1, D_MODEL), const),
                  pl.BlockSpec((D_MODEL, N_SEQ_COLS), const),
                  pl.BlockSpec((tm, ATT_W), tab), pl.BlockSpec((tm, ATT_W), tab)],
        out_specs=out_specs,
        out_shape=out_shape,
        compiler_params=_cparams(1),
        name="inproj",
    )(x, g_mix, w_seq, cos_t, sin_t)


def _lambda_value(lam_ref):
    lam = lam_ref[...]
    s01 = jnp.sum(lam[0:1, :] * lam[1:2, :], axis=-1, keepdims=True)
    s23 = jnp.sum(lam[2:3, :] * lam[3:4, :], axis=-1, keepdims=True)
    return jnp.exp(s01) - jnp.exp(s23)


def _diff_norm(o0, o1, lam_val, g_sub, lam_init):
    od = o0 - lam_val * o1
    return _rms(od, g_sub) * (1.0 - lam_init)


def _split_maps(qh):
    lane = lax.broadcasted_iota(I32, qh.shape, 1)
    zero = jnp.zeros_like(qh)
    return jnp.concatenate([jnp.where(lane < DH_A, qh, zero), jnp.where(lane >= DH_A, qh, zero)], axis=0)


def _attn_prompt_kernel(q_ref, k_ref, v_ref, lam_ref, g_ref, o_ref, q2_scr, *state, lam_init):
    i = pl.program_id(1)
    tq = q_ref.shape[1]
    chains = [(h, c0) for h in range(H_A) for c0 in range(0, 2 * tq, QUERY_GROUP)]
    m_scrs, l_scrs, acc_scrs = state[0::3], state[1::3], state[2::3]
    feat = lax.broadcasted_iota(I32, (E_A, tq), 0)
    for h in range(H_A):
        qh = q_ref[h * E_A:(h + 1) * E_A, :].astype(F32)
        q2_scr[h] = jnp.concatenate([jnp.where(feat < DH_A, qh, 0.0), jnp.where(feat >= DH_A, qh, 0.0)],
                                    axis=1).astype(BF16)
    for n in range(len(chains)):
        m_scrs[n][...] = jnp.full(m_scrs[n].shape, NEG_INF, F32)
        l_scrs[n][...] = jnp.zeros(l_scrs[n].shape, F32)
        acc_scrs[n][...] = jnp.zeros(acc_scrs[n].shape, F32)

    def key_blocks(blocks):
        starts = [pl.multiple_of(kb * tq, tq) for kb, _ in blocks]
        items = [(b, n) for b in range(len(blocks)) for n in range(len(chains))]

        def n_keys(b, c0):
            return min(tq, c0 % tq + QUERY_GROUP) if blocks[b][1] else tq

        def scores(item):
            b, n = item
            h, c0 = chains[n]
            nk = n_keys(b, c0)
            s = _dot(k_ref[pl.ds(starts[b], nk), h * E_A:(h + 1) * E_A], q2_scr[h, :, c0:c0 + QUERY_GROUP])
            if blocks[b][1]:
                key = lax.broadcasted_iota(I32, (nk, QUERY_GROUP), 0)
                qry = (lax.broadcasted_iota(I32, (nk, QUERY_GROUP), 1) + c0) % tq
                s = jnp.where(key > qry, NEG_INF, s)
            return s

        ahead = [scores(it) for it in items[:SCORE_LOOKAHEAD]]
        for w, (b, n) in enumerate(items):
            h, c0 = chains[n]
            cs = slice(h * E_A, (h + 1) * E_A)
            s = ahead.pop(0)
            if w + SCORE_LOOKAHEAD < len(items):
                ahead.append(scores(items[w + SCORE_LOOKAHEAD]))
            m_old = m_scrs[n][...]
            m_new = jnp.maximum(m_old, jnp.max(s, axis=0, keepdims=True))
            alpha = jnp.exp2(m_old - m_new)
            p = jnp.exp2(s - m_new)
            l_scrs[n][...] = l_scrs[n][...] * alpha + jnp.sum(p, axis=0, keepdims=True)
            acc_scrs[n][...] = (acc_scrs[n][...] * alpha
                                + _dot(v_ref[cs, pl.ds(starts[b], n_keys(b, c0))], p.astype(BF16)))
            m_scrs[n][...] = m_new

    def below_diagonal_pair(jj, carry):
        key_blocks([(2 * jj, False), (2 * jj + 1, False)])
        return carry

    lax.fori_loop(0, i // 2, below_diagonal_pair, 0)

    @pl.when(i % 2 == 1)
    def _():
        key_blocks([(i - 1, False), (i, True)])

    @pl.when(i % 2 == 0)
    def _():
        key_blocks([(i, True)])

    lam_val = _lambda_value(lam_ref) + lam_init
    per_head = len(chains) // H_A
    for h in range(H_A):
        o = jnp.concatenate([acc_scrs[n][...] / l_scrs[n][...] for n in range(h * per_head, (h + 1) * per_head)],
                            axis=1)
        od = o[:, :tq] - lam_val * o[:, tq:]
        ms = jnp.mean(od * od, axis=0, keepdims=True)
        y_t = od * lax.rsqrt(ms + EPS) * g_ref[...] * (1.0 - lam_init)
        o_ref[:, h * E_A:(h + 1) * E_A] = y_t.T.astype(o_ref.dtype)


def _attn_prompt(q_t, k, v_t, lam, g_sub_col, batch, seq, lam_init):
    tq = min(ATTN_TILE, seq)
    nq = seq // tq
    n_chains = H_A * (2 * tq // QUERY_GROUP)
    state = [pltpu.VMEM((1, QUERY_GROUP), F32), pltpu.VMEM((1, QUERY_GROUP), F32),
             pltpu.VMEM((E_A, QUERY_GROUP), F32)] * n_chains
    return pl.pallas_call(
        functools.partial(_attn_prompt_kernel, lam_init=lam_init),
        grid=(batch, nq),
        in_specs=[pl.BlockSpec((None, ATT_W, tq), lambda b, i: (b, 0, i)),
                  pl.BlockSpec((seq, ATT_W), lambda b, i: (b, 0)),
                  pl.BlockSpec((None, ATT_W, seq), lambda b, i: (b, 0, 0)),
                  pl.BlockSpec((4, DH_A), lambda b, i: (0, 0)),
                  pl.BlockSpec((E_A, 1), lambda b, i: (0, 0))],
        out_specs=pl.BlockSpec((tq, ATT_W), lambda b, i: (b * nq + i, 0)),
        out_shape=jax.ShapeDtypeStruct(k.shape, BF16),
        scratch_shapes=[pltpu.VMEM((H_A, E_A, 2 * tq), BF16)] + state,
        compiler_params=_cparams(2),
        name="attn_prompt",
    )(q_t, k, v_t, lam, g_sub_col)


def _attn_sample_kernel(pt_ref, q_ref, kn_ref, vn_ref, lam_ref, g_ref, *refs, lam_init, pages):
    k_refs = refs[:pages]
    v_refs = refs[pages:2 * pages]
    o_ref = refs[2 * pages]
    m_scr, l_scr, acc_scr = refs[2 * pages + 1:]
    j = pl.program_id(1)
    t = q_ref.shape[0]

    @pl.when(j == 0)
    def _():
        m_scr[...] = jnp.full(m_scr.shape, NEG_INF, F32)
        l_scr[...] = jnp.zeros(l_scr.shape, F32)
        acc_scr[...] = jnp.zeros(acc_scr.shape, F32)

    rows_h = 2 * t

    def own_head_blocks(big):
        return jnp.concatenate([big[h * rows_h:(h + 1) * rows_h, h * E_A:(h + 1) * E_A] for h in range(H_A)], axis=0)

    def update(s, v_all):
        m_old = m_scr[...]
        m_new = jnp.maximum(m_old, jnp.max(s, axis=-1, keepdims=True))
        alpha = jnp.exp(m_old - m_new)
        p = jnp.exp(s - m_new)
        l_scr[...] = l_scr[...] * alpha + jnp.sum(p, axis=-1, keepdims=True)
        acc_scr[...] = acc_scr[...] * alpha + own_head_blocks(_dot(p.astype(BF16), v_all))
        m_scr[...] = m_new

    q = q_ref[...]
    zero = jnp.zeros((rows_h, E_A), F32)
    qall = jnp.concatenate(
        [jnp.concatenate([_split_maps(q[:, h * E_A:(h + 1) * E_A]) if hh == h else zero for hh in range(H_A)], axis=1)
         for h in range(H_A)], axis=0).astype(BF16)

    kt = jnp.concatenate([k_refs[r][...].astype(BF16) for r in range(pages)], axis=1)
    v_all = jnp.concatenate(
        [jnp.concatenate([v_refs[r][pl.ds(h, PAGE_SIZE, stride=H_A), :] for h in range(H_A)], axis=1)
         for r in range(pages)], axis=0).astype(BF16)
    update(_dot(qall, kt), v_all)

    @pl.when(j == pl.num_programs(1) - 1)
    def _():
        lam_val = _lambda_value(lam_ref) + lam_init
        n_rows = H_A * rows_h
        row_t = lax.broadcasted_iota(I32, (n_rows, PAGE_SIZE), 0) % t
        col = lax.broadcasted_iota(I32, (n_rows, PAGE_SIZE), 1)
        visible = col <= row_t
        pad = jnp.zeros((PAGE_SIZE - t, ATT_W), F32)
        kn_t = jnp.concatenate([kn_ref[...], pad], axis=0).T.astype(BF16)
        vn = jnp.concatenate([vn_ref[...], pad], axis=0).astype(BF16)
        update(jnp.where(visible, _dot(qall, kn_t), NEG_INF), vn)
        o = acc_scr[...] / l_scr[...]
        for h in range(H_A):
            o0 = o[h * rows_h:h * rows_h + t]
            o1 = o[h * rows_h + t:(h + 1) * rows_h]
            y = _diff_norm(o0, o1, lam_val, g_ref[...], lam_init)
            o_ref[:, h * E_A:(h + 1) * E_A] = y.astype(BF16).astype(o_ref.dtype)


def _attn_sample(q, kn, vn, cache_kt, cache_v2, page_table, lam, g_sub, batch, t, lam_init):
    pages = PAGES_PER_STEP
    n_pages = page_table.shape[1]
    steps = n_pages // pages
    fixed = lambda b, j, pt: (0, 0)
    rowb = lambda b, j, pt: (b, 0)
    n_rows = H_A * 2 * t

    def page_spec(r):
        return pl.BlockSpec((None, ATT_W, PAGE_SIZE), lambda b, j, pt: (pt[b, j * pages + r], 0, 0))

    grid_spec = pltpu.PrefetchScalarGridSpec(
        num_scalar_prefetch=1,
        grid=(batch, steps),
        in_specs=[pl.BlockSpec((t, ATT_W), rowb), pl.BlockSpec((t, ATT_W), rowb), pl.BlockSpec((t, ATT_W), rowb),
                  pl.BlockSpec((4, DH_A), fixed), pl.BlockSpec((1, E_A), fixed)]
                 + [page_spec(r) for r in range(pages)] + [page_spec(r) for r in range(pages)],
        out_specs=pl.BlockSpec((t, ATT_W), rowb),
        scratch_shapes=[pltpu.VMEM((n_rows, 1), F32), pltpu.VMEM((n_rows, 1), F32), pltpu.VMEM((n_rows, E_A), F32)],
    )
    return pl.pallas_call(
        functools.partial(_attn_sample_kernel, lam_init=lam_init, pages=pages),
        grid_spec=grid_spec,
        out_shape=jax.ShapeDtypeStruct(q.shape, F32),
        compiler_params=_cparams(2),
        name="attn_sample",
    )(page_table, q, kn, vn, lam, g_sub, *([cache_kt] * pages), *([cache_v2] * pages))


def _split3(x):
    a = x.astype(BF16)
    r = x - a.astype(F32)
    b = r.astype(BF16)
    c = (r - b.astype(F32)).astype(BF16)
    return a, b, c


def _hgrn_kernel(qr_ref, fl_ref, ir_ref, gr_ref, lb_ref, g_ref, s0_ref, y_ref, s_ref,
                 qe_scr, kk_scr, dl_scr, o_scr, st_scr, *, chunk, blk):
    t = qr_ref.shape[0]
    lb = lb_ref[...]
    row = lax.broadcasted_iota(I32, (blk, blk), 0)
    col = lax.broadcasted_iota(I32, (blk, blk), 1)
    same = (row // chunk) == (col // chunk)
    causal = jnp.logical_and(same, col <= row)
    cum_mask = jnp.where(causal, 1.0, 0.0).astype(BF16)
    all_mask = jnp.where(same, 1.0, 0.0).astype(BF16)

    def gates(b, carry):
        rs = pl.ds(pl.multiple_of(b * blk, blk), blk)
        f = lb + (1.0 - lb) * jax.nn.sigmoid(fl_ref[rs, :])
        g = jnp.log(f)
        k = 1.0 - f
        g1, g2, g3 = _split3(g)
        bcum = _dot(cum_mask, g1) + _dot(cum_mask, g2) + _dot(cum_mask, g3)
        blast = _dot(all_mask, g1) + _dot(all_mask, g2) + _dot(all_mask, g3)
        qe = (qr_ref[rs, :] * jnp.exp(bcum)).astype(BF16)
        kd = (k * jnp.exp(-bcum)).astype(BF16)
        qe_scr[rs, :] = qe.astype(qe_scr.dtype)
        kk_scr[rs, :] = (k * jnp.exp(blast - bcum)).astype(kk_scr.dtype)
        dl_scr[rs, :] = jnp.exp(blast)
        v = ir_ref[rs, :].astype(BF16)
        heads = [slice(h * K_R, (h + 1) * K_R) for h in range(H_R)]
        a = [jnp.where(causal, _dot_nt(qe[:, cs], kd[:, cs]), 0.0).astype(BF16) for cs in heads]
        for cs, a_h in zip(heads, a):
            o_scr[rs, cs] = _dot(a_h, v[:, cs])
        return carry

    lax.fori_loop(0, t // blk, gates, 0, unroll=2 if t // blk >= 2 else 1)

    for h in range(H_R):
        st_scr[h] = s0_ref[h].T

    n_chunks = t // chunk
    per_step = 2 if n_chunks % 2 == 0 else 1
    heads = [slice(h * K_R, (h + 1) * K_R) for h in range(H_R)]

    def step(j, carry):
        starts = [pl.multiple_of((j * per_step + u) * chunk, chunk) for u in range(per_step)]
        incs = [[_dot(ir_ref[pl.ds(r0, chunk), cs].T.astype(BF16), kk_scr[pl.ds(r0, chunk), cs].astype(BF16))
                 for cs in heads] for r0 in starts]
        for r0, inc in zip(starts, incs):
            rows = pl.ds(r0, chunk)
            for h, cs in enumerate(heads):
                st = st_scr[h]
                o_scr[rows, cs] = o_scr[rows, cs] + _dot_nt(qe_scr[rows, cs].astype(BF16), st.astype(BF16))
                st_scr[h] = st * dl_scr[pl.ds(r0, 1), cs] + inc[h]
        return carry

    lax.fori_loop(0, n_chunks // per_step, step, 0)
    for h in range(H_R):
        s_ref[h] = st_scr[h].T

    g_rec = g_ref[...]

    def finish(b, carry):
        rs = pl.ds(pl.multiple_of(b * blk, blk), blk)
        for h in range(H_R):
            cs = slice(h * K_R, (h + 1) * K_R)
            y = _rms(o_scr[rs, cs], g_rec) * _silu(gr_ref[rs, cs])
            y_ref[rs, cs] = y.astype(BF16).astype(y_ref.dtype)
        return carry

    lax.fori_loop(0, t // blk, finish, 0)


def _hgrn(qr, fl, ir, gr, lb, g_rec, s0, batch, t, out_dtype):
    chunk = math.gcd(t, HGRN_CHUNK)
    blk = min(t, 256)
    rowb = lambda b: (b, 0)
    fixed = lambda b: (0, 0)
    seq = pl.BlockSpec((t, REC_W), rowb)
    state = pl.BlockSpec((None, H_R, K_R, V_R), lambda b: (b, 0, 0, 0))
    return pl.pallas_call(
        functools.partial(_hgrn_kernel, chunk=chunk, blk=blk),
        grid=(batch,),
        in_specs=[seq, seq, seq, seq, pl.BlockSpec((1, REC_W), fixed), pl.BlockSpec((1, V_R), fixed), state],
        out_specs=[seq, state],
        out_shape=[jax.ShapeDtypeStruct(qr.shape, out_dtype), jax.ShapeDtypeStruct(s0.shape, F32)],
        scratch_shapes=[pltpu.VMEM((t, REC_W), out_dtype), pltpu.VMEM((t, REC_W), out_dtype),
                        pltpu.VMEM((t, REC_W), F32), pltpu.VMEM((t, REC_W), F32), pltpu.VMEM((H_R, V_R, K_R), F32)],
        compiler_params=_cparams(1),
        name="hgrn",
    )(qr, fl, ir, gr, lb, g_rec, s0)


def _mix_kernel(x_ref, ya_ref, yr_ref, gmix_ref, wga_ref, wgr_ref, wba_ref, wbr_ref, wout_ref,
                gffn_ref, wrt_ref, brt_ref, xr_ref, cnt_ref, cnt_scr):
    i = pl.program_id(0)
    tm = x_ref.shape[0]

    @pl.when(i == 0)
    def _():
        cnt_scr[...] = jnp.zeros(cnt_scr.shape, F32)

    part = min(tm, MIX_PART)
    parts = [slice(r, r + part) for r in range(0, tm, part)]
    xs = [x_ref[rs, :] for rs in parts]
    hs = [_rms(x, gmix_ref[...]).astype(BF16) for x in xs]
    gate_a = [jax.nn.sigmoid(_dot(h, wga_ref[...])) for h in hs]
    br_a = [_dot(ya_ref[rs, :], wba_ref[...]) for rs in parts]
    gate_r = [jax.nn.sigmoid(_dot(h, wgr_ref[...])) for h in hs]
    br_r = [_dot(yr_ref[rs, :], wbr_ref[...]) for rs in parts]
    merged = [(ga * ba + gr * br).astype(BF16) for ga, ba, gr, br in zip(gate_a, br_a, gate_r, br_r)]
    x1s = [x + _dot(m, wout_ref[...]) for x, m in zip(xs, merged)]
    for rs, x1 in zip(parts, x1s):
        xr_ref[rs, :D_MODEL] = x1
    hfs = [_rms(x1, gffn_ref[...]).astype(BF16) for x1 in x1s]
    lgs = [_dot(hf, wrt_ref[...]) + brt_ref[...] for hf in hfs]
    for rs, lg in zip(parts, lgs):
        xr_ref[rs, D_MODEL:] = _route_rows(lg, cnt_scr)
    cnt_ref[...] = cnt_scr[...]


def _route_rows(lg, cnt_scr):
    tm = lg.shape[0]
    lane = lax.broadcasted_iota(I32, (tm, LANES), 1)
    is_group = lane < N_GROUPS
    mg = jnp.max(jnp.where(is_group, lg, NEG_INF), axis=-1, keepdims=True)
    gidx = jnp.min(jnp.where(jnp.logical_and(is_group, lg == mg), lane, LANES), axis=-1, keepdims=True)
    p_top = 1.0 / jnp.sum(jnp.where(is_group, jnp.exp(lg - mg), 0.0), axis=-1, keepdims=True)
    base = N_GROUPS + EXPERTS_PER_GROUP * gidx
    in_grp = jnp.logical_and(lane >= base, lane < base + EXPERTS_PER_GROUP)
    v1 = jnp.max(jnp.where(in_grp, lg, NEG_INF), axis=-1, keepdims=True)
    e1 = jnp.min(jnp.where(jnp.logical_and(in_grp, lg == v1), lane, LANES), axis=-1, keepdims=True)
    rest = jnp.logical_and(in_grp, lane != e1)
    v2 = jnp.max(jnp.where(rest, lg, NEG_INF), axis=-1, keepdims=True)
    e2 = jnp.min(jnp.where(jnp.logical_and(rest, lg == v2), lane, LANES), axis=-1, keepdims=True)
    tt = jnp.exp(v2 - v1)
    w_a = (1.0 / (1.0 + tt)) * p_top
    w_b = (tt / (1.0 + tt)) * p_top
    a = e1 - base
    b = e2 - base
    a_first = a < b
    lo = jnp.minimum(a, b)
    hi = jnp.maximum(a, b)
    w_lo = jnp.where(a_first, w_a, w_b)
    w_hi = jnp.where(a_first, w_b, w_a)
    pair = jnp.where(lo == 0, hi - 1, jnp.where(lo == 1, hi + 1, 5))
    cls = gidx * N_PAIRS + pair

    onehot = lane == cls
    r_i = lax.broadcasted_iota(I32, (tm, tm), 0)
    c_i = lax.broadcasted_iota(I32, (tm, tm), 1)
    before = jnp.where(c_i < r_i, 1.0, 0.0).astype(BF16)
    excl = _dot(before, jnp.where(onehot, 1.0, 0.0).astype(BF16))
    rank = jnp.sum(jnp.where(onehot, excl + cnt_scr[...], 0.0), axis=-1, keepdims=True)
    cnt_scr[...] = cnt_scr[...] + jnp.sum(jnp.where(onehot, 1.0, 0.0), axis=0, keepdims=True)
    return jnp.where(lane == 0, cls.astype(F32),
                     jnp.where(lane == 1, w_lo, jnp.where(lane == 2, w_hi, jnp.where(lane == 3, rank, 0.0))))


def _mix(x, ya, yr, g_mix, wga, wgr, wba, wbr, wout, g_ffn, wrt, brt):
    n = x.shape[0]
    tm = min(MIX_TILE, n)
    row = lambda i: (i, 0)
    fixed = lambda i: (0, 0)
    full = lambda a: pl.BlockSpec(a.shape, fixed)
    return pl.pallas_call(
        _mix_kernel,
        grid=(n // tm,),
        in_specs=[pl.BlockSpec((tm, D_MODEL), row), pl.BlockSpec((tm, ATT_W), row), pl.BlockSpec((tm, REC_W), row),
                  full(g_mix), full(wga), full(wgr), full(wba), full(wbr), full(wout), full(g_ffn), full(wrt), full(brt)],
        out_specs=[pl.BlockSpec((tm, ROW_W), row), pl.BlockSpec((1, LANES), fixed)],
        out_shape=[jax.ShapeDtypeStruct((n, ROW_W), F32), jax.ShapeDtypeStruct((1, LANES), F32)],
        scratch_shapes=[pltpu.VMEM((1, LANES), F32)],
        compiler_params=_cparams(1),
        name="mix",
    )(x, ya, yr, g_mix, wga, wgr, wba, wbr, wout, g_ffn, wrt, brt)


def _row_copy_out(x_ref, o_hbm, sem, r, p):
    return pltpu.make_async_copy(x_ref.at[pl.ds(r, 1)], o_hbm.at[pl.ds(p, 1)], sem)


def _dispatch_kernel(zs_ref, pos_ref, x_ref, o_hbm, zero_scr, zsem, rsem):
    i = pl.program_id(0)
    tm = x_ref.shape[0]

    def zero_copy(c):
        start = pl.multiple_of(jnp.maximum(zs_ref[c], 0), tm)
        return pltpu.make_async_copy(zero_scr, o_hbm.at[pl.ds(start, tm)], zsem)

    @pl.when(i == 0)
    def _():
        zero_scr[...] = jnp.zeros(zero_scr.shape, F32)
        for c in range(zs_ref.shape[0]):
            @pl.when(zs_ref[c] >= 0)
            def _():
                zero_copy(c).start()
        for c in range(zs_ref.shape[0]):
            @pl.when(zs_ref[c] >= 0)
            def _():
                zero_copy(c).wait()

    for r in range(tm):
        _row_copy_out(x_ref, o_hbm, rsem, r, pos_ref[0, r]).start(priority=r % 2)
    pltpu.make_async_copy(x_ref, o_hbm.at[pl.ds(0, tm)], rsem).wait()


def _dispatch(xr, pos2d, zero_starts, n_rows_sorted):
    n = xr.shape[0]
    tm = TOKEN_TILE
    grid_spec = pltpu.PrefetchScalarGridSpec(
        num_scalar_prefetch=1,
        grid=(n // tm,),
        in_specs=[pl.BlockSpec((None, 1, tm), lambda i, zs: (i, 0, 0), memory_space=pltpu.SMEM),
                  pl.BlockSpec((tm, ROW_W), lambda i, zs: (i, 0))],
        out_specs=pl.BlockSpec(memory_space=pl.ANY),
        scratch_shapes=[pltpu.VMEM((tm, ROW_W), F32), pltpu.SemaphoreType.DMA(()), pltpu.SemaphoreType.DMA(())],
    )
    return pl.pallas_call(
        _dispatch_kernel,
        grid_spec=grid_spec,
        out_shape=jax.ShapeDtypeStruct((n_rows_sorted, ROW_W), F32),
        compiler_params=_cparams(1),
        name="dispatch",
    )(zero_starts, pos2d, xr)


MOE_TILES_PER_STEP = 2


def _moe_kernel(blk_ref, elo_ref, ehi_ref, valid_ref, g_ref, *refs):
    j = pl.program_id(0)
    k = MOE_TILES_PER_STEP
    xs_refs = refs[:k]
    w_refs = [refs[k + 6 * u:k + 6 * (u + 1)] for u in range(k)]
    o_ref = refs[7 * k]
    tm = xs_refs[0].shape[0]
    xs = [r[:, :D_MODEL] for r in xs_refs]
    hs = [_rms(x, g_ref[...]).astype(BF16) for x in xs]
    a_lo = [_dot(h, w[0][...]) for h, w in zip(hs, w_refs)]
    u_lo = [_dot(h, w[2][...]) for h, w in zip(hs, w_refs)]
    a_hi = [_dot(h, w[1][...]) for h, w in zip(hs, w_refs)]
    u_hi = [_dot(h, w[3][...]) for h, w in zip(hs, w_refs)]
    hid_lo = [(_silu(a) * u * r[:, D_MODEL + 1:D_MODEL + 2]).astype(BF16) for a, u, r in zip(a_lo, u_lo, xs_refs)]
    hid_hi = [(_silu(a) * u * r[:, D_MODEL + 2:D_MODEL + 3]).astype(BF16) for a, u, r in zip(a_hi, u_hi, xs_refs)]
    ys = [_dot(hl, w[4][...]) + _dot(hh, w[5][...]) for hl, hh, w in zip(hid_lo, hid_hi, w_refs)]
    for u in range(k):
        o_ref[u * tm:(u + 1) * tm, :] = jnp.where(valid_ref[k * j + u] == 1, xs[u] + ys[u], 0.0)


def _moe(xs, g_ffn, wg, wu, wd, in_blk, e_lo, e_hi, valid):
    tm = TOKEN_TILE
    k = MOE_TILES_PER_STEP
    n_tiles = xs.shape[0] // tm

    def tile_specs(u):
        lo = lambda j, blk, elo, ehi, va: (elo[k * j + u], 0, 0)
        hi = lambda j, blk, elo, ehi, va: (ehi[k * j + u], 0, 0)
        up = pl.BlockSpec((None, D_MODEL, D_EXPERT), lo), pl.BlockSpec((None, D_MODEL, D_EXPERT), hi)
        down = pl.BlockSpec((None, D_EXPERT, D_MODEL), lo), pl.BlockSpec((None, D_EXPERT, D_MODEL), hi)
        return [up[0], up[1], up[0], up[1], down[0], down[1]]

    def rows_spec(u):
        return pl.BlockSpec((tm, ROW_W), lambda j, blk, elo, ehi, va: (blk[k * j + u], 0))

    grid_spec = pltpu.PrefetchScalarGridSpec(
        num_scalar_prefetch=4,
        grid=(n_tiles // k,),
        in_specs=[pl.BlockSpec((1, D_MODEL), lambda j, blk, elo, ehi, va: (0, 0))]
                 + [rows_spec(u) for u in range(k)] + [s for u in range(k) for s in tile_specs(u)],
        out_specs=pl.BlockSpec((k * tm, D_MODEL), lambda j, blk, elo, ehi, va: (j, 0)),
    )
    return pl.pallas_call(
        _moe_kernel,
        grid_spec=grid_spec,
        out_shape=jax.ShapeDtypeStruct((xs.shape[0], D_MODEL), F32),
        compiler_params=_cparams(1),
        name="moe",
    )(in_blk, e_lo, e_hi, valid, g_ffn, *([xs] * k), *([wg, wg, wu, wu, wd, wd] * k))


def _ple_kernel(pos_ref, pos_next_ref, xs_hbm, p_ref, gple_ref, wpg_ref, wp_ref, gfin_ref, y_ref, buf, sem):
    i = pl.program_id(0)
    tm = buf.shape[1]
    slot = i % 2

    def gather(idx_ref, s):
        for r in range(tm):
            pltpu.make_async_copy(xs_hbm.at[pl.ds(idx_ref[0, r], 1)], buf.at[s, pl.ds(r, 1)],
                                  sem.at[s]).start(priority=r % 2)

    @pl.when(i == 0)
    def _():
        gather(pos_ref, 0)

    @pl.when(i + 1 < pl.num_programs(0))
    def _():
        gather(pos_next_ref, 1 - slot)

    pltpu.make_async_copy(xs_hbm.at[pl.ds(0, tm)], buf.at[slot], sem.at[slot]).wait()

    x2 = buf[slot]
    hp = _rms(x2, gple_ref[...]).astype(BF16)
    gate = jax.nn.sigmoid(_dot(hp, wpg_ref[...]))
    x3 = x2 + gate * _dot(p_ref[...].astype(BF16), wp_ref[...])
    y_ref[...] = _rms(x3, gfin_ref[...])


def _ple(xs2, pos2d, p, g_ple, wpg, wp, g_final):
    n = p.shape[0]
    tm = TOKEN_TILE
    row = lambda i: (i, 0)
    fixed = lambda i: (0, 0)
    last = n // tm - 1
    return pl.pallas_call(
        _ple_kernel,
        grid=(n // tm,),
        in_specs=[pl.BlockSpec((None, 1, tm), lambda i: (i, 0, 0), memory_space=pltpu.SMEM),
                  pl.BlockSpec((None, 1, tm), lambda i: (jnp.minimum(i + 1, last), 0, 0), memory_space=pltpu.SMEM),
                  pl.BlockSpec(memory_space=pl.ANY),
                  pl.BlockSpec((tm, P_DIM), row), pl.BlockSpec((1, D_MODEL), fixed),
                  pl.BlockSpec((D_MODEL, D_MODEL), fixed), pl.BlockSpec((P_DIM, D_MODEL), fixed),
                  pl.BlockSpec((1, D_MODEL), fixed)],
        out_specs=pl.BlockSpec((tm, D_MODEL), row),
        out_shape=jax.ShapeDtypeStruct((n, D_MODEL), F32),
        scratch_shapes=[pltpu.VMEM((2, tm, D_MODEL), F32), pltpu.SemaphoreType.DMA((2,))],
        compiler_params=_cparams(1),
        name="ple",
    )(pos2d, pos2d, xs2, p, g_ple, wpg, wp, g_final)


def _positions_kernel(off_ref, route_ref, pos_ref):
    n_sub, _, tm = pos_ref.shape
    lane = lax.broadcasted_iota(I32, (tm, LANES), 1)
    for u in range(n_sub):
        route = route_ref[u * tm:(u + 1) * tm, :]
        cls = route[:, 0:1].astype(I32)
        first = jnp.sum(jnp.where(lane == cls, off_ref[...], 0.0), axis=-1, keepdims=True)
        pos = first + route[:, 3:4]
        pos_ref[u] = jnp.where(lane == 0, pos, 0.0).T[0:1, :].astype(I32)


def _positions(xr, class_row_start, n):
    tm = TOKEN_TILE
    n_sub = min(POSITION_TILES, n // tm)
    return pl.pallas_call(
        _positions_kernel,
        grid=(n // (tm * n_sub),),
        in_specs=[pl.BlockSpec((1, LANES), lambda i: (0, 0)),
                  pl.BlockSpec((tm * n_sub, LANES), lambda i: (i, D_MODEL // LANES))],
        out_specs=pl.BlockSpec((n_sub, 1, tm), lambda i: (i, 0, 0)),
        out_shape=jax.ShapeDtypeStruct((n // tm, 1, tm), I32),
        compiler_params=_cparams(1),
        name="positions",
    )(class_row_start, xr)


def _routing_plan(xr, counts, n):
    tm = TOKEN_TILE
    n_tiles = pl.cdiv(n // tm + N_CLASSES, MOE_TILES_PER_STEP) * MOE_TILES_PER_STEP
    cnt = counts[0, :N_CLASSES].astype(I32)
    tiles = (cnt + tm - 1) // tm
    tile_end = jnp.cumsum(tiles)
    tile_start = tile_end - tiles
    n_used = tile_end[-1]
    class_row_start = jnp.pad((tile_start * tm).astype(F32), (0, LANES - N_CLASSES)).reshape(1, LANES)
    pos3d = _positions(xr, class_row_start, n)
    j = jnp.arange(n_tiles, dtype=I32)
    valid = j < n_used
    in_blk = jnp.minimum(j, n_used - 1)
    tcls = jnp.sum((tile_end[None, :] <= in_blk[:, None]).astype(I32), axis=1)
    grp = tcls // N_PAIRS
    pair = tcls % N_PAIRS
    e_lo = grp * EXPERTS_PER_GROUP + jnp.asarray(PAIR_LO, I32)[pair]
    e_hi = grp * EXPERTS_PER_GROUP + jnp.asarray(PAIR_HI, I32)[pair]
    seg_zero = jnp.where(tiles > 0, tile_end * tm - tm, -1)
    tail = n_used + jnp.arange(n_tiles - n // tm, dtype=I32)
    tail_zero = jnp.where(tail < n_tiles, tail * tm, -1)
    zero_starts = jnp.concatenate([seg_zero, tail_zero]).astype(I32)
    return pos3d, zero_starts, in_blk, e_lo, e_hi, valid.astype(I32), n_tiles * tm


def _rope_tables(pos, rows):
    half = DH_A // 2
    inv_freq = ROPE_THETA ** (-jnp.arange(half, dtype=F32) / half)
    ang = pos.astype(F32)[:, None] * inv_freq[None, :]
    cos = jnp.cos(ang)
    sin = jnp.sin(ang)
    cos_t = jnp.tile(jnp.concatenate([cos, cos], axis=-1), (1, ATT_W // DH_A))
    sin_t = jnp.tile(jnp.concatenate([-sin, sin], axis=-1), (1, ATT_W // DH_A))
    reps = max(rows // pos.shape[0], 1)
    return jnp.tile(cos_t, (reps, 1)), jnp.tile(sin_t, (reps, 1))


def _layer(i, x, p, pos, attend, s0, w, batch, seq, transposed):
    n = batch * seq
    small = BF16 if seq % 16 == 0 else F32
    cos_t, sin_t = _rope_tables(pos, TOKEN_TILE)
    proj = _inproj(x, w["g_mix"], w["w_seq"], cos_t, sin_t, batch, seq, transposed)
    lam_init = 0.8 - 0.6 * math.exp(-0.3 * i)
    ya, kf, vf = attend(proj, lam_init)
    qr, fl, ir, gr = proj[-4:]
    yr, s_new = _hgrn(qr, fl, ir, gr, w["lb"], w["g_rec"], s0, batch, seq, small)
    xr, counts = _mix(x, ya.astype(BF16), yr.astype(BF16), w["g_mix"], w["w_gate_a"], w["w_gate_r"], w["w_branch_a"],
                      w["w_branch_r"], w["w_out"], w["g_ffn"], w["w_route"], w["b_route"])
    pos2d, zero_starts, in_blk, e_lo, e_hi, valid, n_sorted = _routing_plan(xr, counts, n)
    xs = _dispatch(xr, pos2d, zero_starts, n_sorted)
    xs2 = _moe(xs, w["g_ffn"], w["w_exp_gate"], w["w_exp_up"], w["w_exp_down"], in_blk, e_lo, e_hi, valid)
    y = _ple(xs2, pos2d, p, w["g_ple"], w["w_ple_gate"], w["w_ple"], w["g_final"])
    return y, kf, vf, s_new


def kernel(x_prompt, x_sample, p_prompt, p_sample, cache_k, cache_v, state_hgrn, page_table, g_mix, w_in, lam,
           g_subln, lb_param, g_rec, w_branch_a, w_branch_r, w_out, g_ffn, w_route_group, b_route_group,
           w_route_expert, b_route_expert, w_exp_gate, w_exp_up, w_exp_down, g_ple, w_ple_gate, w_ple, g_final):
    depth = w_in.shape[0]
    assert depth == 1, "single-layer step"
    bp, sp, _ = x_prompt.shape
    bs, ts, _ = x_sample.shape
    past_len = page_table.shape[1] * PAGE_SIZE
    i = 0

    w_in_b = w_in[i].astype(BF16)
    n_route = N_GROUPS + N_EXPERTS
    w_route = jnp.concatenate([w_route_group[i], w_route_expert[i]], axis=1)
    w_route = jnp.pad(w_route, ((0, 0), (0, LANES - n_route))).astype(BF16)
    b_route = jnp.pad(jnp.concatenate([b_route_group[i], b_route_expert[i]]), (0, LANES - n_route)).reshape(1, LANES)
    lb = jnp.cumsum(jax.nn.softmax(lb_param.astype(F32), axis=0), axis=0)[i].reshape(1, REC_W)
    w = dict(
        g_mix=g_mix[i].reshape(1, D_MODEL), w_seq=w_in_b[:, :N_SEQ_COLS],
        w_gate_a=w_in_b[:, N_SEQ_COLS:N_SEQ_COLS + D_MODEL], w_gate_r=w_in_b[:, N_SEQ_COLS + D_MODEL:],
        lb=lb, g_rec=g_rec[i].reshape(1, V_R),
        w_branch_a=w_branch_a[i].astype(BF16), w_branch_r=w_branch_r[i].astype(BF16), w_out=w_out[i].astype(BF16),
        g_ffn=g_ffn[i].reshape(1, D_MODEL), w_route=w_route, b_route=b_route,
        w_exp_gate=w_exp_gate[i].astype(BF16), w_exp_up=w_exp_up[i].astype(BF16), w_exp_down=w_exp_down[i].astype(BF16),
        g_ple=g_ple[i].reshape(1, D_MODEL), w_ple_gate=w_ple_gate[i].astype(BF16), w_ple=w_ple[i].astype(BF16),
        g_final=g_final.reshape(1, D_MODEL),
    )
    lam_i = lam[i].astype(F32)
    g_sub = g_subln[i].reshape(1, E_A)

    def attend_prompt(proj, lam_init):
        q_t, k_t, kb, vf, v_t = proj[:5]
        return _attn_prompt(q_t, kb, v_t, lam_i, g_sub.reshape(E_A, 1), bp, sp, lam_init), k_t, vf

    n_phys = cache_k.shape[1]
    ckt = jnp.transpose(cache_k[i], (0, 2, 3, 4, 1)).reshape(n_phys, ATT_W, PAGE_SIZE)
    cv2 = cache_v[i].reshape(n_phys, PAGE_SIZE * H_A, E_A)

    def attend_sample(proj, lam_init):
        q, kf, vf = proj[:3]
        return _attn_sample(q, kf, vf, ckt, cv2, page_table, lam_i, g_sub, bs, ts, lam_init), kf, vf

    pos_p = jnp.arange(sp, dtype=I32)
    pos_s = past_len + jnp.arange(ts, dtype=I32)
    s0_p = jnp.zeros((bp, H_R, K_R, V_R), F32)

    y_p, k_t, v_p, s_p = _layer(i, x_prompt.reshape(bp * sp, D_MODEL), p_prompt[i].reshape(bp * sp, P_DIM),
                                pos_p, attend_prompt, s0_p, w, bp, sp, True)
    y_s, k_s, v_s, s_s = _layer(i, x_sample.reshape(bs * ts, D_MODEL), p_sample[i].reshape(bs * ts, P_DIM),
                                pos_s, attend_sample, state_hgrn[i], w, bs, ts, False)
    k_p = jnp.transpose(k_t.reshape(1, bp, H_A, 2, DH_A, sp), (0, 1, 5, 2, 3, 4))

    return (y_p.reshape(bp, sp, D_MODEL), y_s.reshape(bs, ts, D_MODEL),
            k_p, v_p.reshape(1, bp, sp, H_A, E_A), s_p.reshape(1, bp, H_R, K_R, V_R),
            k_s.reshape(1, bs, ts, H_A, 2, DH_A), v_s.reshape(1, bs, ts, H_A, E_A), s_s.reshape(1, bs, H_R, K_R, V_R))
```

```python
import functools
import math

import jax
import jax.numpy as jnp
import numpy as np
from jax import lax
from jax.experimental import pallas as pl
from jax.experimental.pallas import tpu as pltpu

F32 = jnp.float32
BF16 = jnp.bfloat16
I32 = jnp.int32

D_MODEL = 1024
P_DIM = 256
H_A = 4
DH_A = 64
E_A = 2 * DH_A
H_R = 4
K_R = 128
V_R = 128
HGRN_CHUNK = 32
N_GROUPS = 4
EXPERTS_PER_GROUP = 4
N_EXPERTS = N_GROUPS * EXPERTS_PER_GROUP
D_EXPERT = 256
ROPE_THETA = 10000.0
EPS = 1e-6
NEG_INF = -1e30
LOG2_E = 1.4426950408889634
PAGE_SIZE = 128
ATT_W = H_A * E_A
REC_W = H_R * V_R
N_SEQ_COLS = 7 * 512
LANES = 128
N_PAIRS = 6
N_CLASSES = N_GROUPS * N_PAIRS
PAIR_LO = (0, 0, 0, 1, 1, 2)
PAIR_HI = (1, 2, 3, 2, 3, 3)
ROW_W = D_MODEL + LANES
TOKEN_TILE = 256
INPROJ_TILE = 512
POSITION_TILES = 8
MIX_TILE = 1024
MIX_PART = 512
ATTN_TILE = 512
QUERY_GROUP = 256
SCORE_LOOKAHEAD = 4
PAGES_PER_STEP = 32
VMEM_LIMIT = 56 * 1024 * 1024


def _cparams(n_axes, flags=None):
    return pltpu.CompilerParams(dimension_semantics=("arbitrary",) * n_axes, vmem_limit_bytes=VMEM_LIMIT, flags=flags)


def _rms(x, g):
    return x * lax.rsqrt(jnp.mean(x * x, axis=-1, keepdims=True) + EPS) * g


def _silu(x):
    return x * jax.nn.sigmoid(x)


def _dot(a, b):
    return jnp.dot(a, b, preferred_element_type=F32)


def _dot_nt(a, b):
    return lax.dot_general(a, b, (((1,), (1,)), ((), ())), preferred_element_type=F32)


def _inproj_kernel(x_ref, g_ref, w_ref, cos_ref, sin_ref, *out_refs, transposed):
    h = _rms(x_ref[...], g_ref[...]).astype(BF16)
    cos = cos_ref[...]
    sin = sin_ref[...]
    tm = h.shape[0]
    lane = lax.broadcasted_iota(I32, (tm, ATT_W), 1)
    first_half = (lane % DH_A) < (DH_A // 2)

    def proj(c):
        return _dot(h, w_ref[:, c * 512:(c + 1) * 512])

    def rope(t):
        rot = jnp.where(first_half, pltpu.roll(t, ATT_W - DH_A // 2, 1), pltpu.roll(t, DH_A // 2, 1))
        return t * cos + rot * sin

    q = rope(proj(0)) * (DH_A ** -0.5)
    k = rope(proj(1))
    v = proj(2)
    if transposed:
        qt_ref, kt_ref, kb_ref, vf_ref, vt_ref = out_refs[:5]
        qt_ref[...] = (q * LOG2_E).T.astype(BF16)
        kt_ref[...] = k.T
        kb_ref[...] = k.astype(BF16)
        for head in range(H_A):
            vf_ref[pl.ds(head, tm, stride=H_A), :] = v[:, head * E_A:(head + 1) * E_A]
        vt_ref[...] = v.T.astype(BF16)
        rest = out_refs[5:]
    else:
        q_ref, kf_ref, vf_ref = out_refs[:3]
        q_ref[...] = q
        kf_ref[...] = k
        vf_ref[...] = v
        rest = out_refs[3:]
    qr_ref, fl_ref, ir_ref, gr_ref = rest
    qr_ref[...] = _silu(proj(3))
    fl_ref[...] = proj(4)
    ir_ref[...] = proj(5)
    gr_ref[...] = proj(6)


def _inproj(x, g_mix, w_seq, cos_t, sin_t, batch, seq, transposed):
    n = x.shape[0]
    tm = cos_t.shape[0] if seq < TOKEN_TILE else min(INPROJ_TILE, seq)
    nblk = max(seq // tm, 1)
    row = lambda i: (i, 0)
    tab = lambda i: (i % nblk, 0)
    const = lambda i: (0, 0)
    o512 = pl.BlockSpec((tm, 512), row)
    sds = lambda dt: jax.ShapeDtypeStruct((n, 512), dt)
    if transposed:
        ot = pl.BlockSpec((None, 512, tm), lambda i: (i // nblk, 0, i % nblk))
        sdt = lambda dt: jax.ShapeDtypeStruct((batch, 512, seq), dt)
        ov = pl.BlockSpec((tm * H_A, E_A), row)
        out_specs = [ot, ot, o512, ov, ot] + [o512] * 4
        out_shape = ([sdt(BF16), sdt(F32), sds(BF16), jax.ShapeDtypeStruct((n * H_A, E_A), F32), sdt(BF16)]
                     + [sds(F32)] * 4)
    else:
        out_specs = [o512] * 7
        out_shape = [sds(F32)] * 7
    return pl.pallas_call(
        functools.partial(_inproj_kernel, transposed=transposed),
        grid=(n // tm,),
        in_specs=[pl.BlockSpec((tm, D_MODEL), row), pl.BlockSpec((1, D_MODEL), const),
                  pl.BlockSpec((D_MODEL, N_SEQ_COLS), const),
                  pl.BlockSpec((tm, ATT_W), tab), pl.BlockSpec((tm, ATT_W), tab)],
        out_specs=out_specs,
        out_shape=out_shape,
        compiler_params=_cparams(1),
        name="inproj",
    )(x, g_mix, w_seq, cos_t, sin_t)


def _lambda_value(lam_ref):
    lam = lam_ref[...]
    s01 = jnp.sum(lam[0:1, :] * lam[1:2, :], axis=-1, keepdims=True)
    s23 = jnp.sum(lam[2:3, :] * lam[3:4, :], axis=-1, keepdims=True)
    return jnp.exp(s01) - jnp.exp(s23)


def _diff_norm(o0, o1, lam_val, g_sub, lam_init):
    od = o0 - lam_val * o1
    return _rms(od, g_sub) * (1.0 - lam_init)


def _split_maps(qh):
    lane = lax.broadcasted_iota(I32, qh.shape, 1)
    zero = jnp.zeros_like(qh)
    return jnp.concatenate([jnp.where(lane < DH_A, qh, zero), jnp.where(lane >= DH_A, qh, zero)], axis=0)


def _attn_prompt_kernel(q_ref, k_ref, v_ref, lam_ref, g_ref, o_ref, q2_scr, *state, lam_init):
    i = pl.program_id(1)
    tq = q_ref.shape[1]
    chains = [(h, c0) for h in range(H_A) for c0 in range(0, 2 * tq, QUERY_GROUP)]
    m_scrs, l_scrs, acc_scrs = state[0::3], state[1::3], state[2::3]
    feat = lax.broadcasted_iota(I32, (E_A, tq), 0)
    for h in range(H_A):
        qh = q_ref[h * E_A:(h + 1) * E_A, :].astype(F32)
        q2_scr[h] = jnp.concatenate([jnp.where(feat < DH_A, qh, 0.0), jnp.where(feat >= DH_A, qh, 0.0)],
                                    axis=1).astype(BF16)
    for n in range(len(chains)):
        m_scrs[n][...] = jnp.full(m_scrs[n].shape, NEG_INF, F32)
        l_scrs[n][...] = jnp.zeros(l_scrs[n].shape, F32)
        acc_scrs[n][...] = jnp.zeros(acc_scrs[n].shape, F32)

    def key_blocks(blocks):
        starts = [pl.multiple_of(kb * tq, tq) for kb, _ in blocks]
        items = [(b, n) for b in range(len(blocks)) for n in range(len(chains))]

        def n_keys(b, c0):
            return min(tq, c0 % tq + QUERY_GROUP) if blocks[b][1] else tq

        def scores(item):
            b, n = item
            h, c0 = chains[n]
            nk = n_keys(b, c0)
            s = _dot(k_ref[pl.ds(starts[b], nk), h * E_A:(h + 1) * E_A], q2_scr[h, :, c0:c0 + QUERY_GROUP])
            if blocks[b][1]:
                key = lax.broadcasted_iota(I32, (nk, QUERY_GROUP), 0)
                qry = (lax.broadcasted_iota(I32, (nk, QUERY_GROUP), 1) + c0) % tq
                s = jnp.where(key > qry, NEG_INF, s)
            return s

        ahead = [scores(it) for it in items[:SCORE_LOOKAHEAD]]
        for w, (b, n) in enumerate(items):
            h, c0 = chains[n]
            cs = slice(h * E_A, (h + 1) * E_A)
            s = ahead.pop(0)
            if w + SCORE_LOOKAHEAD < len(items):
                ahead.append(scores(items[w + SCORE_LOOKAHEAD]))
            m_old = m_scrs[n][...]
            m_new = jnp.maximum(m_old, jnp.max(s, axis=0, keepdims=True))
            alpha = jnp.exp2(m_old - m_new)
            p = jnp.exp2(s - m_new)
            l_scrs[n][...] = l_scrs[n][...] * alpha + jnp.sum(p, axis=0, keepdims=True)
            acc_scrs[n][...] = (acc_scrs[n][...] * alpha
                                + _dot(v_ref[cs, pl.ds(starts[b], n_keys(b, c0))], p.astype(BF16)))
            m_scrs[n][...] = m_new

    def below_diagonal_pair(jj, carry):
        key_blocks([(2 * jj, False), (2 * jj + 1, False)])
        return carry

    lax.fori_loop(0, i // 2, below_diagonal_pair, 0)

    @pl.when(i % 2 == 1)
    def _():
        key_blocks([(i - 1, False), (i, True)])

    @pl.when(i % 2 == 0)
    def _():
        key_blocks([(i, True)])

    lam_val = _lambda_value(lam_ref) + lam_init
    per_head = len(chains) // H_A
    for h in range(H_A):
        o = jnp.concatenate([acc_scrs[n][...] / l_scrs[n][...] for n in range(h * per_head, (h + 1) * per_head)],
                            axis=1)
        od = o[:, :tq] - lam_val * o[:, tq:]
        ms = jnp.mean(od * od, axis=0, keepdims=True)
        y_t = od * lax.rsqrt(ms + EPS) * g_ref[...] * (1.0 - lam_init)
        o_ref[:, h * E_A:(h + 1) * E_A] = y_t.T.astype(o_ref.dtype)


def _attn_prompt(q_t, k, v_t, lam, g_sub_col, batch, seq, lam_init):
    tq = min(ATTN_TILE, seq)
    nq = seq // tq
    n_chains = H_A * (2 * tq // QUERY_GROUP)
    state = [pltpu.VMEM((1, QUERY_GROUP), F32), pltpu.VMEM((1, QUERY_GROUP), F32),
             pltpu.VMEM((E_A, QUERY_GROUP), F32)] * n_chains
    return pl.pallas_call(
        functools.partial(_attn_prompt_kernel, lam_init=lam_init),
        grid=(batch, nq),
        in_specs=[pl.BlockSpec((None, ATT_W, tq), lambda b, i: (b, 0, i)),
                  pl.BlockSpec((seq, ATT_W), lambda b, i: (b, 0)),
                  pl.BlockSpec((None, ATT_W, seq), lambda b, i: (b, 0, 0)),
                  pl.BlockSpec((4, DH_A), lambda b, i: (0, 0)),
                  pl.BlockSpec((E_A, 1), lambda b, i: (0, 0))],
        out_specs=pl.BlockSpec((tq, ATT_W), lambda b, i: (b * nq + i, 0)),
        out_shape=jax.ShapeDtypeStruct(k.shape, BF16),
        scratch_shapes=[pltpu.VMEM((H_A, E_A, 2 * tq), BF16)] + state,
        compiler_params=_cparams(2),
        name="attn_prompt",
    )(q_t, k, v_t, lam, g_sub_col)


def _attn_sample_kernel(pt_ref, q_ref, kn_ref, vn_ref, lam_ref, g_ref, *refs, lam_init, pages):
    k_refs = refs[:pages]
    v_refs = refs[pages:2 * pages]
    o_ref = refs[2 * pages]
    m_scr, l_scr, acc_scr = refs[2 * pages + 1:]
    j = pl.program_id(1)
    t = q_ref.shape[0]

    @pl.when(j == 0)
    def _():
        m_scr[...] = jnp.full(m_scr.shape, NEG_INF, F32)
        l_scr[...] = jnp.zeros(l_scr.shape, F32)
        acc_scr[...] = jnp.zeros(acc_scr.shape, F32)

    rows_h = 2 * t

    def own_head_blocks(big):
        return jnp.concatenate([big[h * rows_h:(h + 1) * rows_h, h * E_A:(h + 1) * E_A] for h in range(H_A)], axis=0)

    def update(s, v_all):
        m_old = m_scr[...]
        m_new = jnp.maximum(m_old, jnp.max(s, axis=-1, keepdims=True))
        alpha = jnp.exp(m_old - m_new)
        p = jnp.exp(s - m_new)
        l_scr[...] = l_scr[...] * alpha + jnp.sum(p, axis=-1, keepdims=True)
        acc_scr[...] = acc_scr[...] * alpha + own_head_blocks(_dot(p.astype(BF16), v_all))
        m_scr[...] = m_new

    q = q_ref[...]
    zero = jnp.zeros((rows_h, E_A), F32)
    qall = jnp.concatenate(
        [jnp.concatenate([_split_maps(q[:, h * E_A:(h + 1) * E_A]) if hh == h else zero for hh in range(H_A)], axis=1)
         for h in range(H_A)], axis=0).astype(BF16)

    kt = jnp.concatenate([k_refs[r][...].astype(BF16) for r in range(pages)], axis=1)
    v_all = jnp.concatenate(
        [jnp.concatenate([v_refs[r][pl.ds(h, PAGE_SIZE, stride=H_A), :] for h in range(H_A)], axis=1)
         for r in range(pages)], axis=0).astype(BF16)
    update(_dot(qall, kt), v_all)

    @pl.when(j == pl.num_programs(1) - 1)
    def _():
        lam_val = _lambda_value(lam_ref) + lam_init
        n_rows = H_A * rows_h
        row_t = lax.broadcasted_iota(I32, (n_rows, PAGE_SIZE), 0) % t
        col = lax.broadcasted_iota(I32, (n_rows, PAGE_SIZE), 1)
        visible = col <= row_t
        pad = jnp.zeros((PAGE_SIZE - t, ATT_W), F32)
        kn_t = jnp.concatenate([kn_ref[...], pad], axis=0).T.astype(BF16)
        vn = jnp.concatenate([vn_ref[...], pad], axis=0).astype(BF16)
        update(jnp.where(visible, _dot(qall, kn_t), NEG_INF), vn)
        o = acc_scr[...] / l_scr[...]
        for h in range(H_A):
            o0 = o[h * rows_h:h * rows_h + t]
            o1 = o[h * rows_h + t:(h + 1) * rows_h]
            y = _diff_norm(o0, o1, lam_val, g_ref[...], lam_init)
            o_ref[:, h * E_A:(h + 1) * E_A] = y.astype(BF16).astype(o_ref.dtype)


def _attn_sample(q, kn, vn, cache_kt, cache_v2, page_table, lam, g_sub, batch, t, lam_init):
    pages = PAGES_PER_STEP
    n_pages = page_table.shape[1]
    steps = n_pages // pages
    fixed = lambda b, j, pt: (0, 0)
    rowb = lambda b, j, pt: (b, 0)
    n_rows = H_A * 2 * t

    def page_spec(r):
        return pl.BlockSpec((None, ATT_W, PAGE_SIZE), lambda b, j, pt: (pt[b, j * pages + r], 0, 0))

    grid_spec = pltpu.PrefetchScalarGridSpec(
        num_scalar_prefetch=1,
        grid=(batch, steps),
        in_specs=[pl.BlockSpec((t, ATT_W), rowb), pl.BlockSpec((t, ATT_W), rowb), pl.BlockSpec((t, ATT_W), rowb),
                  pl.BlockSpec((4, DH_A), fixed), pl.BlockSpec((1, E_A), fixed)]
                 + [page_spec(r) for r in range(pages)] + [page_spec(r) for r in range(pages)],
        out_specs=pl.BlockSpec((t, ATT_W), rowb),
        scratch_shapes=[pltpu.VMEM((n_rows, 1), F32), pltpu.VMEM((n_rows, 1), F32), pltpu.VMEM((n_rows, E_A), F32)],
    )
    return pl.pallas_call(
        functools.partial(_attn_sample_kernel, lam_init=lam_init, pages=pages),
        grid_spec=grid_spec,
        out_shape=jax.ShapeDtypeStruct(q.shape, F32),
        compiler_params=_cparams(2),
        name="attn_sample",
    )(page_table, q, kn, vn, lam, g_sub, *([cache_kt] * pages), *([cache_v2] * pages))


def _split3(x):
    a = x.astype(BF16)
    r = x - a.astype(F32)
    b = r.astype(BF16)
    c = (r - b.astype(F32)).astype(BF16)
    return a, b, c


def _hgrn_kernel(qr_ref, fl_ref, ir_ref, gr_ref, lb_ref, g_ref, s0_ref, y_ref, s_ref,
                 qe_scr, kk_scr, dl_scr, o_scr, st_scr, *, chunk, blk):
    t = qr_ref.shape[0]
    lb = lb_ref[...]
    row = lax.broadcasted_iota(I32, (blk, blk), 0)
    col = lax.broadcasted_iota(I32, (blk, blk), 1)
    same = (row // chunk) == (col // chunk)
    causal = jnp.logical_and(same, col <= row)
    cum_mask = jnp.where(causal, 1.0, 0.0).astype(BF16)
    all_mask = jnp.where(same, 1.0, 0.0).astype(BF16)

    def gates(b, carry):
        rs = pl.ds(pl.multiple_of(b * blk, blk), blk)
        f = lb + (1.0 - lb) * jax.nn.sigmoid(fl_ref[rs, :])
        g = jnp.log(f)
        k = 1.0 - f
        g1, g2, g3 = _split3(g)
        bcum = _dot(cum_mask, g1) + _dot(cum_mask, g2) + _dot(cum_mask, g3)
        blast = _dot(all_mask, g1) + _dot(all_mask, g2) + _dot(all_mask, g3)
        qe = (qr_ref[rs, :] * jnp.exp(bcum)).astype(BF16)
        kd = (k * jnp.exp(-bcum)).astype(BF16)
        qe_scr[rs, :] = qe.astype(qe_scr.dtype)
        kk_scr[rs, :] = (k * jnp.exp(blast - bcum)).astype(kk_scr.dtype)
        dl_scr[rs, :] = jnp.exp(blast)
        v = ir_ref[rs, :].astype(BF16)
        heads = [slice(h * K_R, (h + 1) * K_R) for h in range(H_R)]
        a = [jnp.where(causal, _dot_nt(qe[:, cs], kd[:, cs]), 0.0).astype(BF16) for cs in heads]
        for cs, a_h in zip(heads, a):
            o_scr[rs, cs] = _dot(a_h, v[:, cs])
        return carry

    lax.fori_loop(0, t // blk, gates, 0, unroll=2 if t // blk >= 2 else 1)

    for h in range(H_R):
        st_scr[h] = s0_ref[h].T

    n_chunks = t // chunk
    per_step = 2 if n_chunks % 2 == 0 else 1
    heads = [slice(h * K_R, (h + 1) * K_R) for h in range(H_R)]

    def step(j, carry):
        starts = [pl.multiple_of((j * per_step + u) * chunk, chunk) for u in range(per_step)]
        incs = [[lax.dot_general(ir_ref[pl.ds(r0, chunk), cs].astype(BF16), kk_scr[pl.ds(r0, chunk), cs].astype(BF16),
                                 (((0,), (0,)), ((), ())), preferred_element_type=F32)
                 for cs in heads] for r0 in starts]
        for r0, inc in zip(starts, incs):
            rows = pl.ds(r0, chunk)
            for h, cs in enumerate(heads):
                st = st_scr[h]
                o_scr[rows, cs] = o_scr[rows, cs] + _dot_nt(qe_scr[rows, cs].astype(BF16), st.astype(BF16))
                st_scr[h] = st * dl_scr[pl.ds(r0, 1), cs] + inc[h]
        return carry

    lax.fori_loop(0, n_chunks // per_step, step, 0)
    for h in range(H_R):
        s_ref[h] = st_scr[h].T

    g_rec = g_ref[...]

    def finish(b, carry):
        rs = pl.ds(pl.multiple_of(b * blk, blk), blk)
        for h in range(H_R):
            cs = slice(h * K_R, (h + 1) * K_R)
            y = _rms(o_scr[rs, cs], g_rec) * _silu(gr_ref[rs, cs])
            y_ref[rs, cs] = y.astype(BF16).astype(y_ref.dtype)
        return carry

    lax.fori_loop(0, t // blk, finish, 0)


def _hgrn(qr, fl, ir, gr, lb, g_rec, s0, batch, t, out_dtype):
    chunk = math.gcd(t, HGRN_CHUNK)
    blk = min(t, 256)
    rowb = lambda b: (b, 0)
    fixed = lambda b: (0, 0)
    seq = pl.BlockSpec((t, REC_W), rowb)
    state = pl.BlockSpec((None, H_R, K_R, V_R), lambda b: (b, 0, 0, 0))
    return pl.pallas_call(
        functools.partial(_hgrn_kernel, chunk=chunk, blk=blk),
        grid=(batch,),
        in_specs=[seq, seq, seq, seq, pl.BlockSpec((1, REC_W), fixed), pl.BlockSpec((1, V_R), fixed), state],
        out_specs=[seq, state],
        out_shape=[jax.ShapeDtypeStruct(qr.shape, out_dtype), jax.ShapeDtypeStruct(s0.shape, F32)],
        scratch_shapes=[pltpu.VMEM((t, REC_W), out_dtype), pltpu.VMEM((t, REC_W), out_dtype),
                        pltpu.VMEM((t, REC_W), F32), pltpu.VMEM((t, REC_W), F32), pltpu.VMEM((H_R, V_R, K_R), F32)],
        compiler_params=_cparams(1),
        name="hgrn",
    )(qr, fl, ir, gr, lb, g_rec, s0)


def _mix_kernel(x_ref, ya_ref, yr_ref, gmix_ref, wga_ref, wgr_ref, wba_ref, wbr_ref, wout_ref,
                gffn_ref, wrt_ref, brt_ref, xr_ref, cnt_ref, cnt_scr):
    i = pl.program_id(0)
    tm = x_ref.shape[0]

    @pl.when(i == 0)
    def _():
        cnt_scr[...] = jnp.zeros(cnt_scr.shape, F32)

    part = min(tm, MIX_PART)
    parts = [slice(r, r + part) for r in range(0, tm, part)]
    xs = [x_ref[rs, :] for rs in parts]
    hs = [_rms(x, gmix_ref[...]).astype(BF16) for x in xs]
    gate_a = [jax.nn.sigmoid(_dot(h, wga_ref[...])) for h in hs]
    br_a = [_dot(ya_ref[rs, :], wba_ref[...]) for rs in parts]
    gate_r = [jax.nn.sigmoid(_dot(h, wgr_ref[...])) for h in hs]
    br_r = [_dot(yr_ref[rs, :], wbr_ref[...]) for rs in parts]
    merged = [(ga * ba + gr * br).astype(BF16) for ga, ba, gr, br in zip(gate_a, br_a, gate_r, br_r)]
    x1s = [x + _dot(m, wout_ref[...]) for x, m in zip(xs, merged)]
    for rs, x1 in zip(parts, x1s):
        xr_ref[rs, :D_MODEL] = x1
    hfs = [_rms(x1, gffn_ref[...]).astype(BF16) for x1 in x1s]
    lgs = [_dot(hf, wrt_ref[...]) + brt_ref[...] for hf in hfs]
    for rs, lg in zip(parts, lgs):
        xr_ref[rs, D_MODEL:] = _route_rows(lg, cnt_scr)
    cnt_ref[...] = cnt_scr[...]


def _route_rows(lg, cnt_scr):
    tm = lg.shape[0]
    lane = lax.broadcasted_iota(I32, (tm, LANES), 1)
    is_group = lane < N_GROUPS
    mg = jnp.max(jnp.where(is_group, lg, NEG_INF), axis=-1, keepdims=True)
    gidx = jnp.min(jnp.where(jnp.logical_and(is_group, lg == mg), lane, LANES), axis=-1, keepdims=True)
    p_top = 1.0 / jnp.sum(jnp.where(is_group, jnp.exp(lg - mg), 0.0), axis=-1, keepdims=True)
    base = N_GROUPS + EXPERTS_PER_GROUP * gidx
    in_grp = jnp.logical_and(lane >= base, lane < base + EXPERTS_PER_GROUP)
    v1 = jnp.max(jnp.where(in_grp, lg, NEG_INF), axis=-1, keepdims=True)
    e1 = jnp.min(jnp.where(jnp.logical_and(in_grp, lg == v1), lane, LANES), axis=-1, keepdims=True)
    rest = jnp.logical_and(in_grp, lane != e1)
    v2 = jnp.max(jnp.where(rest, lg, NEG_INF), axis=-1, keepdims=True)
    e2 = jnp.min(jnp.where(jnp.logical_and(rest, lg == v2), lane, LANES), axis=-1, keepdims=True)
    tt = jnp.exp(v2 - v1)
    w_a = (1.0 / (1.0 + tt)) * p_top
    w_b = (tt / (1.0 + tt)) * p_top
    a = e1 - base
    b = e2 - base
    a_first = a < b
    lo = jnp.minimum(a, b)
    hi = jnp.maximum(a, b)
    w_lo = jnp.where(a_first, w_a, w_b)
    w_hi = jnp.where(a_first, w_b, w_a)
    pair = jnp.where(lo == 0, hi - 1, jnp.where(lo == 1, hi + 1, 5))
    cls = gidx * N_PAIRS + pair

    onehot = lane == cls
    r_i = lax.broadcasted_iota(I32, (tm, tm), 0)
    c_i = lax.broadcasted_iota(I32, (tm, tm), 1)
    before = jnp.where(c_i < r_i, 1.0, 0.0).astype(BF16)
    excl = _dot(before, jnp.where(onehot, 1.0, 0.0).astype(BF16))
    rank = jnp.sum(jnp.where(onehot, excl + cnt_scr[...], 0.0), axis=-1, keepdims=True)
    cnt_scr[...] = cnt_scr[...] + jnp.sum(jnp.where(onehot, 1.0, 0.0), axis=0, keepdims=True)
    return jnp.where(lane == 0, cls.astype(F32),
                     jnp.where(lane == 1, w_lo, jnp.where(lane == 2, w_hi, jnp.where(lane == 3, rank, 0.0))))


def _mix(x, ya, yr, g_mix, wga, wgr, wba, wbr, wout, g_ffn, wrt, brt):
    n = x.shape[0]
    tm = min(MIX_TILE, n)
    row = lambda i: (i, 0)
    fixed = lambda i: (0, 0)
    full = lambda a: pl.BlockSpec(a.shape, fixed)
    return pl.pallas_call(
        _mix_kernel,
        grid=(n // tm,),
        in_specs=[pl.BlockSpec((tm, D_MODEL), row), pl.BlockSpec((tm, ATT_W), row), pl.BlockSpec((tm, REC_W), row),
                  full(g_mix), full(wga), full(wgr), full(wba), full(wbr), full(wout), full(g_ffn), full(wrt), full(brt)],
        out_specs=[pl.BlockSpec((tm, ROW_W), row), pl.BlockSpec((1, LANES), fixed)],
        out_shape=[jax.ShapeDtypeStruct((n, ROW_W), F32), jax.ShapeDtypeStruct((1, LANES), F32)],
        scratch_shapes=[pltpu.VMEM((1, LANES), F32)],
        compiler_params=_cparams(1),
        name="mix",
    )(x, ya, yr, g_mix, wga, wgr, wba, wbr, wout, g_ffn, wrt, brt)


def _row_copy_out(x_ref, o_hbm, sem, r, p):
    return pltpu.make_async_copy(x_ref.at[pl.ds(r, 1)], o_hbm.at[pl.ds(p, 1)], sem)


def _dispatch_kernel(zs_ref, pos_ref, x_ref, o_hbm, zero_scr, zsem, rsem):
    i = pl.program_id(0)
    tm = x_ref.shape[0]

    def zero_copy(c):
        start = pl.multiple_of(jnp.maximum(zs_ref[c], 0), tm)
        return pltpu.make_async_copy(zero_scr, o_hbm.at[pl.ds(start, tm)], zsem)

    @pl.when(i == 0)
    def _():
        zero_scr[...] = jnp.zeros(zero_scr.shape, F32)
        for c in range(zs_ref.shape[0]):
            @pl.when(zs_ref[c] >= 0)
            def _():
                zero_copy(c).start()
        for c in range(zs_ref.shape[0]):
            @pl.when(zs_ref[c] >= 0)
            def _():
                zero_copy(c).wait()

    for r in range(tm):
        _row_copy_out(x_ref, o_hbm, rsem, r, pos_ref[0, r]).start(priority=r % 2)
    pltpu.make_async_copy(x_ref, o_hbm.at[pl.ds(0, tm)], rsem).wait()


def _dispatch(xr, pos2d, zero_starts, n_rows_sorted):
    n = xr.shape[0]
    tm = TOKEN_TILE
    grid_spec = pltpu.PrefetchScalarGridSpec(
        num_scalar_prefetch=1,
        grid=(n // tm,),
        in_specs=[pl.BlockSpec((None, 1, tm), lambda i, zs: (i, 0, 0), memory_space=pltpu.SMEM),
                  pl.BlockSpec((tm, ROW_W), lambda i, zs: (i, 0))],
        out_specs=pl.BlockSpec(memory_space=pl.ANY),
        scratch_shapes=[pltpu.VMEM((tm, ROW_W), F32), pltpu.SemaphoreType.DMA(()), pltpu.SemaphoreType.DMA(())],
    )
    return pl.pallas_call(
        _dispatch_kernel,
        grid_spec=grid_spec,
        out_shape=jax.ShapeDtypeStruct((n_rows_sorted, ROW_W), F32),
        compiler_params=_cparams(1),
        name="dispatch",
    )(zero_starts, pos2d, xr)


MOE_TILES_PER_STEP = 4


def _moe_kernel(blk_ref, elo_ref, ehi_ref, valid_ref, g_ref, *refs):
    j = pl.program_id(0)
    k = MOE_TILES_PER_STEP
    xs_refs = refs[:k]
    w_refs = [refs[k + 6 * u:k + 6 * (u + 1)] for u in range(k)]
    o_ref = refs[7 * k]
    tm = xs_refs[0].shape[0]
    xs = [r[:, :D_MODEL] for r in xs_refs]
    hs = [_rms(x, g_ref[...]).astype(BF16) for x in xs]
    a_lo = [_dot(h, w[0][...]) for h, w in zip(hs, w_refs)]
    u_lo = [_dot(h, w[2][...]) for h, w in zip(hs, w_refs)]
    a_hi = [_dot(h, w[1][...]) for h, w in zip(hs, w_refs)]
    u_hi = [_dot(h, w[3][...]) for h, w in zip(hs, w_refs)]
    hid_lo = [(_silu(a) * u * r[:, D_MODEL + 1:D_MODEL + 2]).astype(BF16) for a, u, r in zip(a_lo, u_lo, xs_refs)]
    hid_hi = [(_silu(a) * u * r[:, D_MODEL + 2:D_MODEL + 3]).astype(BF16) for a, u, r in zip(a_hi, u_hi, xs_refs)]
    ys = [_dot(hl, w[4][...]) + _dot(hh, w[5][...]) for hl, hh, w in zip(hid_lo, hid_hi, w_refs)]
    for u in range(k):
        o_ref[u * tm:(u + 1) * tm, :] = jnp.where(valid_ref[k * j + u] == 1, xs[u] + ys[u], 0.0)


def _moe(xs, g_ffn, wg, wu, wd, in_blk, e_lo, e_hi, valid):
    tm = TOKEN_TILE
    k = MOE_TILES_PER_STEP
    n_tiles = xs.shape[0] // tm

    def tile_specs(u):
        lo = lambda j, blk, elo, ehi, va: (elo[k * j + u], 0, 0)
        hi = lambda j, blk, elo, ehi, va: (ehi[k * j + u], 0, 0)
        up = pl.BlockSpec((None, D_MODEL, D_EXPERT), lo), pl.BlockSpec((None, D_MODEL, D_EXPERT), hi)
        down = pl.BlockSpec((None, D_EXPERT, D_MODEL), lo), pl.BlockSpec((None, D_EXPERT, D_MODEL), hi)
        return [up[0], up[1], up[0], up[1], down[0], down[1]]

    def rows_spec(u):
        return pl.BlockSpec((tm, ROW_W), lambda j, blk, elo, ehi, va: (blk[k * j + u], 0))

    grid_spec = pltpu.PrefetchScalarGridSpec(
        num_scalar_prefetch=4,
        grid=(n_tiles // k,),
        in_specs=[pl.BlockSpec((1, D_MODEL), lambda j, blk, elo, ehi, va: (0, 0))]
                 + [rows_spec(u) for u in range(k)] + [s for u in range(k) for s in tile_specs(u)],
        out_specs=pl.BlockSpec((k * tm, D_MODEL), lambda j, blk, elo, ehi, va: (j, 0)),
    )
    return pl.pallas_call(
        _moe_kernel,
        grid_spec=grid_spec,
        out_shape=jax.ShapeDtypeStruct((xs.shape[0], D_MODEL), F32),
        compiler_params=_cparams(1),
        name="moe",
    )(in_blk, e_lo, e_hi, valid, g_ffn, *([xs] * k), *([wg, wg, wu, wu, wd, wd] * k))


def _ple_kernel(pos_ref, pos_next_ref, xs_hbm, p_ref, gple_ref, wpg_ref, wp_ref, gfin_ref, y_ref, buf, sem):
    i = pl.program_id(0)
    tm = buf.shape[1]
    slot = i % 2

    def gather(idx_ref, s):
        for r in range(tm):
            pltpu.make_async_copy(xs_hbm.at[pl.ds(idx_ref[0, r], 1)], buf.at[s, pl.ds(r, 1)],
                                  sem.at[s]).start(priority=r % 2)

    @pl.when(i == 0)
    def _():
        gather(pos_ref, 0)

    @pl.when(i + 1 < pl.num_programs(0))
    def _():
        gather(pos_next_ref, 1 - slot)

    pltpu.make_async_copy(xs_hbm.at[pl.ds(0, tm)], buf.at[slot], sem.at[slot]).wait()

    x2 = buf[slot]
    hp = _rms(x2, gple_ref[...]).astype(BF16)
    gate = jax.nn.sigmoid(_dot(hp, wpg_ref[...]))
    x3 = x2 + gate * _dot(p_ref[...].astype(BF16), wp_ref[...])
    y_ref[...] = _rms(x3, gfin_ref[...])


def _ple(xs2, pos2d, p, g_ple, wpg, wp, g_final):
    n = p.shape[0]
    tm = TOKEN_TILE
    row = lambda i: (i, 0)
    fixed = lambda i: (0, 0)
    last = n // tm - 1
    return pl.pallas_call(
        _ple_kernel,
        grid=(n // tm,),
        in_specs=[pl.BlockSpec((None, 1, tm), lambda i: (i, 0, 0), memory_space=pltpu.SMEM),
                  pl.BlockSpec((None, 1, tm), lambda i: (jnp.minimum(i + 1, last), 0, 0), memory_space=pltpu.SMEM),
                  pl.BlockSpec(memory_space=pl.ANY),
                  pl.BlockSpec((tm, P_DIM), row), pl.BlockSpec((1, D_MODEL), fixed),
                  pl.BlockSpec((D_MODEL, D_MODEL), fixed), pl.BlockSpec((P_DIM, D_MODEL), fixed),
                  pl.BlockSpec((1, D_MODEL), fixed)],
        out_specs=pl.BlockSpec((tm, D_MODEL), row),
        out_shape=jax.ShapeDtypeStruct((n, D_MODEL), F32),
        scratch_shapes=[pltpu.VMEM((2, tm, D_MODEL), F32), pltpu.SemaphoreType.DMA((2,))],
        compiler_params=_cparams(1),
        name="ple",
    )(pos2d, pos2d, xs2, p, g_ple, wpg, wp, g_final)


def _positions_kernel(off_ref, route_ref, pos_ref):
    n_sub, _, tm = pos_ref.shape
    lane = lax.broadcasted_iota(I32, (tm, LANES), 1)
    for u in range(n_sub):
        route = route_ref[u * tm:(u + 1) * tm, :]
        cls = route[:, 0:1].astype(I32)
        first = jnp.sum(jnp.where(lane == cls, off_ref[...], 0.0), axis=-1, keepdims=True)
        pos = first + route[:, 3:4]
        pos_ref[u] = jnp.where(lane == 0, pos, 0.0).T[0:1, :].astype(I32)


def _positions(xr, class_row_start, n):
    tm = TOKEN_TILE
    n_sub = min(POSITION_TILES, n // tm)
    return pl.pallas_call(
        _positions_kernel,
        grid=(n // (tm * n_sub),),
        in_specs=[pl.BlockSpec((1, LANES), lambda i: (0, 0)),
                  pl.BlockSpec((tm * n_sub, LANES), lambda i: (i, D_MODEL // LANES))],
        out_specs=pl.BlockSpec((n_sub, 1, tm), lambda i: (i, 0, 0)),
        out_shape=jax.ShapeDtypeStruct((n // tm, 1, tm), I32),
        compiler_params=_cparams(1),
        name="positions",
    )(class_row_start, xr)


def _routing_plan(xr, counts, n):
    tm = TOKEN_TILE
    n_tiles = pl.cdiv(n // tm + N_CLASSES, MOE_TILES_PER_STEP) * MOE_TILES_PER_STEP
    cnt = counts[0, :N_CLASSES].astype(I32)
    tiles = (cnt + tm - 1) // tm
    tile_end = jnp.cumsum(tiles)
    tile_start = tile_end - tiles
    n_used = tile_end[-1]
    class_row_start = jnp.pad((tile_start * tm).astype(F32), (0, LANES - N_CLASSES)).reshape(1, LANES)
    pos3d = _positions(xr, class_row_start, n)
    j = jnp.arange(n_tiles, dtype=I32)
    valid = j < n_used
    in_blk = jnp.minimum(j, n_used - 1)
    tcls = jnp.sum((tile_end[None, :] <= in_blk[:, None]).astype(I32), axis=1)
    grp = tcls // N_PAIRS
    pair = tcls % N_PAIRS
    e_lo = grp * EXPERTS_PER_GROUP + jnp.asarray(PAIR_LO, I32)[pair]
    e_hi = grp * EXPERTS_PER_GROUP + jnp.asarray(PAIR_HI, I32)[pair]
    seg_zero = jnp.where(tiles > 0, tile_end * tm - tm, -1)
    tail = n_used + jnp.arange(n_tiles - n // tm, dtype=I32)
    tail_zero = jnp.where(tail < n_tiles, tail * tm, -1)
    zero_starts = jnp.concatenate([seg_zero, tail_zero]).astype(I32)
    return pos3d, zero_starts, in_blk, e_lo, e_hi, valid.astype(I32), n_tiles * tm


def _rope_tables(pos, rows):
    half = DH_A // 2
    inv_freq = ROPE_THETA ** (-jnp.arange(half, dtype=F32) / half)
    ang = pos.astype(F32)[:, None] * inv_freq[None, :]
    cos = jnp.cos(ang)
    sin = jnp.sin(ang)
    cos_t = jnp.tile(jnp.concatenate([cos, cos], axis=-1), (1, ATT_W // DH_A))
    sin_t = jnp.tile(jnp.concatenate([-sin, sin], axis=-1), (1, ATT_W // DH_A))
    reps = max(rows // pos.shape[0], 1)
    return jnp.tile(cos_t, (reps, 1)), jnp.tile(sin_t, (reps, 1))


def _layer(i, x, p, pos, attend, s0, w, batch, seq, transposed):
    n = batch * seq
    small = BF16 if seq % 16 == 0 else F32
    cos_t, sin_t = _rope_tables(pos, TOKEN_TILE)
    proj = _inproj(x, w["g_mix"], w["w_seq"], cos_t, sin_t, batch, seq, transposed)
    lam_init = 0.8 - 0.6 * math.exp(-0.3 * i)
    ya, kf, vf = attend(proj, lam_init)
    qr, fl, ir, gr = proj[-4:]
    yr, s_new = _hgrn(qr, fl, ir, gr, w["lb"], w["g_rec"], s0, batch, seq, small)
    xr, counts = _mix(x, ya.astype(BF16), yr.astype(BF16), w["g_mix"], w["w_gate_a"], w["w_gate_r"], w["w_branch_a"],
                      w["w_branch_r"], w["w_out"], w["g_ffn"], w["w_route"], w["b_route"])
    pos2d, zero_starts, in_blk, e_lo, e_hi, valid, n_sorted = _routing_plan(xr, counts, n)
    xs = _dispatch(xr, pos2d, zero_starts, n_sorted)
    xs2 = _moe(xs, w["g_ffn"], w["w_exp_gate"], w["w_exp_up"], w["w_exp_down"], in_blk, e_lo, e_hi, valid)
    y = _ple(xs2, pos2d, p, w["g_ple"], w["w_ple_gate"], w["w_ple"], w["g_final"])
    return y, kf, vf, s_new


def kernel(x_prompt, x_sample, p_prompt, p_sample, cache_k, cache_v, state_hgrn, page_table, g_mix, w_in, lam,
           g_subln, lb_param, g_rec, w_branch_a, w_branch_r, w_out, g_ffn, w_route_group, b_route_group,
           w_route_expert, b_route_expert, w_exp_gate, w_exp_up, w_exp_down, g_ple, w_ple_gate, w_ple, g_final):
    depth = w_in.shape[0]
    assert depth == 1, "single-layer step"
    bp, sp, _ = x_prompt.shape
    bs, ts, _ = x_sample.shape
    past_len = page_table.shape[1] * PAGE_SIZE
    i = 0

    w_in_b = w_in[i].astype(BF16)
    n_route = N_GROUPS + N_EXPERTS
    w_route = jnp.concatenate([w_route_group[i], w_route_expert[i]], axis=1)
    w_route = jnp.pad(w_route, ((0, 0), (0, LANES - n_route))).astype(BF16)
    b_route = jnp.pad(jnp.concatenate([b_route_group[i], b_route_expert[i]]), (0, LANES - n_route)).reshape(1, LANES)
    lb = jnp.cumsum(jax.nn.softmax(lb_param.astype(F32), axis=0), axis=0)[i].reshape(1, REC_W)
    w = dict(
        g_mix=g_mix[i].reshape(1, D_MODEL), w_seq=w_in_b[:, :N_SEQ_COLS],
        w_gate_a=w_in_b[:, N_SEQ_COLS:N_SEQ_COLS + D_MODEL], w_gate_r=w_in_b[:, N_SEQ_COLS + D_MODEL:],
        lb=lb, g_rec=g_rec[i].reshape(1, V_R),
        w_branch_a=w_branch_a[i].astype(BF16), w_branch_r=w_branch_r[i].astype(BF16), w_out=w_out[i].astype(BF16),
        g_ffn=g_ffn[i].reshape(1, D_MODEL), w_route=w_route, b_route=b_route,
        w_exp_gate=w_exp_gate[i].astype(BF16), w_exp_up=w_exp_up[i].astype(BF16), w_exp_down=w_exp_down[i].astype(BF16),
        g_ple=g_ple[i].reshape(1, D_MODEL), w_ple_gate=w_ple_gate[i].astype(BF16), w_ple=w_ple[i].astype(BF16),
        g_final=g_final.reshape(1, D_MODEL),
    )
    lam_i = lam[i].astype(F32)
    g_sub = g_subln[i].reshape(1, E_A)

    def attend_prompt(proj, lam_init):
        q_t, k_t, kb, vf, v_t = proj[:5]
        return _attn_prompt(q_t, kb, v_t, lam_i, g_sub.reshape(E_A, 1), bp, sp, lam_init), k_t, vf

    n_phys = cache_k.shape[1]
    ckt = jnp.transpose(cache_k[i], (0, 2, 3, 4, 1)).reshape(n_phys, ATT_W, PAGE_SIZE)
    cv2 = cache_v[i].reshape(n_phys, PAGE_SIZE * H_A, E_A)

    def attend_sample(proj, lam_init):
        q, kf, vf = proj[:3]
        return _attn_sample(q, kf, vf, ckt, cv2, page_table, lam_i, g_sub, bs, ts, lam_init), kf, vf

    pos_p = jnp.arange(sp, dtype=I32)
    pos_s = past_len + jnp.arange(ts, dtype=I32)
    s0_p = jnp.zeros((bp, H_R, K_R, V_R), F32)

    y_p, k_t, v_p, s_p = _layer(i, x_prompt.reshape(bp * sp, D_MODEL), p_prompt[i].reshape(bp * sp, P_DIM),
                                pos_p, attend_prompt, s0_p, w, bp, sp, True)
    y_s, k_s, v_s, s_s = _layer(i, x_sample.reshape(bs * ts, D_MODEL), p_sample[i].reshape(bs * ts, P_DIM),
                                pos_s, attend_sample, state_hgrn[i], w, bs, ts, False)
    k_p = jnp.transpose(k_t.reshape(1, bp, H_A, 2, DH_A, sp), (0, 1, 5, 2, 3, 4))

    return (y_p.reshape(bp, sp, D_MODEL), y_s.reshape(bs, ts, D_MODEL),
            k_p, v_p.reshape(1, bp, sp, H_A, E_A), s_p.reshape(1, bp, H_R, K_R, V_R),
            k_s.reshape(1, bs, ts, H_A, 2, DH_A), v_s.reshape(1, bs, ts, H_A, E_A), s_s.reshape(1, bs, H_R, K_R, V_R))
```
